```python
import math
import jax, jax.numpy as jnp
from jax import lax
import numpy as np

D_MODEL = 1024
BATCH = 8
SEQ = 4096
DEPTH = 2
DEC_BATCH = 8
DEC_SEQ = 32
PAST_LEN = 2048

CHUNK = 64
D_SSM = 512
SSM_GROUP = 16
N_GROUPS = D_SSM // SSM_GROUP
SSM_STATE = 64
N_HEADS = 8
HEAD_DIM = 64
D_ATTN = N_HEADS * HEAD_DIM
LEFT_CHUNKS = 8
BAND = (LEFT_CHUNKS + 1) * CHUNK
REL_CLIP = 128
ATTN_SCALE = HEAD_DIM ** -0.5
N_KEYS = 128
N_EXPERTS = N_KEYS * N_KEYS
PEER_HEADS = 8
PEER_DK = 256
PEER_HALF = PEER_DK // 2
PEER_TOPK = 16
PEER_BLOCK = 128
RMS_EPS = 1e-6
D_IN = D_SSM + 3 * D_ATTN + 2 * D_MODEL
IN_SPLITS = [D_SSM, D_SSM + D_ATTN, D_SSM + 2 * D_ATTN, D_SSM + 3 * D_ATTN, D_SSM + 3 * D_ATTN + D_MODEL]

kernel_name = 'hybrid_s5_chunkband_peer_stream'


def rms_norm(x, g):
    x32 = x.astype(jnp.float32)
    y = x32 * lax.rsqrt(jnp.mean(x32 * x32, axis=-1, keepdims=True) + RMS_EPS)
    return (y * g.astype(jnp.float32)).astype(x.dtype)


def _linear_combine(e1, e2):
    a1, b1 = e1
    a2, b2 = e2
    return a1 * a2, a2 * b1 + b2


def s5_branch(u, h0_re, h0_im, a_re, a_im, log_dt, b_re, b_im, c_re, c_im, d_skip, glu_w):
    f32 = jnp.float32
    bt, l, _ = u.shape
    a = lax.complex(a_re.astype(f32), a_im.astype(f32))
    dt = jnp.exp(log_dt.astype(f32))[:, None]
    a_bar = jnp.exp(a * dt)
    b_bar = ((a_bar - 1.0) / a)[..., None] * lax.complex(b_re.astype(f32), b_im.astype(f32))
    c_mat = lax.complex(c_re.astype(f32), c_im.astype(f32))
    u32 = u.astype(f32)
    ug = u32.reshape(bt, l, N_GROUPS, SSM_GROUP)
    bu = jnp.einsum('gpc,blgc->blgp', b_bar, ug)
    h0 = lax.complex(h0_re.astype(f32), h0_im.astype(f32))
    bu = bu.at[:, 0].add(a_bar * h0)
    a_seq = jnp.broadcast_to(a_bar, bu.shape)
    _, h = lax.associative_scan(_linear_combine, (a_seq, bu), axis=1)
    y = jnp.einsum('gcp,blgp->blgc', c_mat, h).real.reshape(bt, l, D_SSM) + d_skip.astype(f32) * u32
    val, gate = jnp.split(y.astype(u.dtype) @ glu_w, 2, axis=-1)
    h_last = h[:, -1]
    return val * jax.nn.sigmoid(gate), jnp.real(h_last), jnp.imag(h_last)


def rel_bias_lookup(rel_bias, rel):
    idx = jnp.clip(rel, -REL_CLIP, REL_CLIP) + REL_CLIP
    return rel_bias[:, idx].astype(jnp.float32)


def band_attention_prompt(q, k, v, rel_bias):
    bt, s, _ = q.shape
    nc = s // CHUNK
    shp = (bt, nc, CHUNK, N_HEADS, HEAD_DIM)
    qc, kc, vc = q.reshape(shp), k.reshape(shp), v.reshape(shp)
    pad = ((0, 0), (LEFT_CHUNKS, 0), (0, 0), (0, 0), (0, 0))
    kp, vp = jnp.pad(kc, pad), jnp.pad(vc, pad)
    band_idx = jnp.arange(nc)[:, None] + jnp.arange(LEFT_CHUNKS + 1)[None, :]
    kb = kp[:, band_idx].reshape(bt, nc, BAND, N_HEADS, HEAD_DIM)
    vb = vp[:, band_idx].reshape(bt, nc, BAND, N_HEADS, HEAD_DIM)
    valid = jnp.repeat(band_idx >= LEFT_CHUNKS, CHUNK, axis=1)
    rel = (jnp.arange(BAND) - LEFT_CHUNKS * CHUNK)[None, :] - jnp.arange(CHUNK)[:, None]
    bias = rel_bias_lookup(rel_bias, rel)
    scores = jnp.einsum('bnqhd,bnkhd->bnhqk', qc, kb).astype(jnp.float32) * ATTN_SCALE + bias[None, None]
    scores = jnp.where(valid[None, :, None, None, :], scores, jnp.finfo(jnp.float32).min)
    p = jax.nn.softmax(scores, axis=-1).astype(v.dtype)
    o = jnp.einsum('bnhqk,bnkhd->bnqhd', p, vb)
    return o.reshape(bt, s, D_ATTN)


def band_attention_sample(q, k_new, v_new, k_cache, v_cache, rel_bias):
    bt, t, _ = q.shape
    w = k_cache.shape[1]
    qh = q.reshape(bt, t, N_HEADS, HEAD_DIM)
    k_all = jnp.concatenate([k_cache.astype(k_new.dtype), k_new], axis=1)
    v_all = jnp.concatenate([v_cache.astype(v_new.dtype), v_new], axis=1)
    rel = (jnp.arange(w + t) - w)[None, :] - jnp.arange(t)[:, None]
    bias = rel_bias_lookup(rel_bias, rel)
    scores = jnp.einsum('bqhd,bkhd->bhqk', qh, k_all).astype(jnp.float32) * ATTN_SCALE + bias[None]
    p = jax.nn.softmax(scores, axis=-1).astype(v_all.dtype)
    o = jnp.einsum('bhqk,bkhd->bqhd', p, v_all)
    return o.reshape(bt, t, D_ATTN)


def peer_ffn(x, w_q, sub_keys, u_tab, v_tab):
    lead = x.shape[:-1]
    xt = x.reshape(-1, D_MODEL)
    t = xt.shape[0]
    nblk = -(-t // PEER_BLOCK)
    tp = nblk * PEER_BLOCK
    xb_all = jnp.pad(xt, ((0, tp - t), (0, 0))).reshape(nblk, PEER_BLOCK, D_MODEL)

    def block(xb):
        qh = (xb @ w_q).reshape(PEER_BLOCK, PEER_HEADS, 2, PEER_HALF)
        s = jnp.einsum('thzd,hznd->thzn', qh, sub_keys).astype(jnp.float32)
        sv, si = lax.top_k(s, PEER_TOPK)
        cand_s = (sv[:, :, 0, :, None] + sv[:, :, 1, None, :]).reshape(PEER_BLOCK, PEER_HEADS, PEER_TOPK * PEER_TOPK)
        cand_e = (si[:, :, 0, :, None] * N_KEYS + si[:, :, 1, None, :]).reshape(PEER_BLOCK, PEER_HEADS, PEER_TOPK * PEER_TOPK)
        best_s, pos = lax.top_k(cand_s, PEER_TOPK)
        e = jnp.take_along_axis(cand_e, pos, axis=-1)
        g = jax.nn.softmax(best_s, axis=-1).astype(xb.dtype)
        act = jax.nn.gelu(jnp.einsum('td,thkd->thk', xb, u_tab[e]), approximate=False)
        return jnp.einsum('thk,thkd->td', g * act, v_tab[e])

    out = lax.map(block, xb_all).reshape(tp, D_MODEL)[:t]
    return out.reshape(*lead, D_MODEL)


def trunk_layer(x, h0_re, h0_im, k_cache, v_cache, norm1_g, w_in, a_re, a_im, log_dt, b_re, b_im,
                c_re, c_im, d_skip, glu_w, rel_bias, w_o, w_out, norm2_g, pq, psk, pu, pv):
    bt, l, _ = x.shape
    z = rms_norm(x, norm1_g) @ w_in
    u, q, k, v, g_a, g_b = jnp.split(z, IN_SPLITS, axis=-1)
    a_out, h_re, h_im = s5_branch(u, h0_re, h0_im, a_re, a_im, log_dt, b_re, b_im, c_re, c_im, d_skip, glu_w)
    k = k.reshape(bt, l, N_HEADS, HEAD_DIM)
    v = v.reshape(bt, l, N_HEADS, HEAD_DIM)
    if k_cache is None:
        att = band_attention_prompt(q, k, v, rel_bias)
        keep = min(LEFT_CHUNKS * CHUNK, l)
        k_keep, v_keep = k[:, l - keep:], v[:, l - keep:]
    else:
        att = band_attention_sample(q, k, v, k_cache, v_cache, rel_bias)
        k_keep, v_keep = k, v
    b_out = att @ w_o
    mixed = jax.nn.sigmoid(g_a) * a_out + jax.nn.sigmoid(g_b) * b_out
    x = x + mixed @ w_out
    x = x + peer_ffn(rms_norm(x, norm2_g), pq, psk, pu, pv)
    return x, h_re, h_im, k_keep, v_keep


def setup_inputs(seed: int = 0) -> dict:
    key = jax.random.key(seed)
    ks = jax.random.split(key, 26)
    nrm = jax.random.normal
    kv_win = min(LEFT_CHUNKS * CHUNK, PAST_LEN)
    f32 = jnp.float32
    a_im_base = jnp.pi * jnp.arange(SSM_STATE, dtype=f32)
    return {
        'x_prompt': nrm(ks[0], (BATCH, SEQ, D_MODEL), f32),
        'x_sample': nrm(ks[1], (DEC_BATCH, DEC_SEQ, D_MODEL), f32),
        'cache_k': nrm(ks[2], (DEPTH, DEC_BATCH, kv_win, N_HEADS, HEAD_DIM), f32),
        'cache_v': nrm(ks[3], (DEPTH, DEC_BATCH, kv_win, N_HEADS, HEAD_DIM), f32),
        'state_ssm_re': 0.5 * nrm(ks[4], (DEPTH, DEC_BATCH, N_GROUPS, SSM_STATE), f32),
        'state_ssm_im': 0.5 * nrm(ks[5], (DEPTH, DEC_BATCH, N_GROUPS, SSM_STATE), f32),
        'norm1_g': 1.0 + 0.05 * nrm(ks[6], (DEPTH, D_MODEL), f32),
        'w_in': nrm(ks[7], (DEPTH, D_MODEL, D_IN), f32) * D_MODEL ** -0.5,
        'ssm_a_re': -0.5 + 0.01 * nrm(ks[8], (DEPTH, N_GROUPS, SSM_STATE), f32),
        'ssm_a_im': a_im_base + 0.01 * nrm(ks[9], (DEPTH, N_GROUPS, SSM_STATE), f32),
        'ssm_log_dt': jax.random.uniform(ks[10], (DEPTH, N_GROUPS), f32, math.log(1e-3), math.log(1e-1)),
        'ssm_b_re': nrm(ks[11], (DEPTH, N_GROUPS, SSM_STATE, SSM_GROUP), f32) * (2 * SSM_GROUP) ** -0.5,
        'ssm_b_im': nrm(ks[12], (DEPTH, N_GROUPS, SSM_STATE, SSM_GROUP), f32) * (2 * SSM_GROUP) ** -0.5,
        'ssm_c_re': nrm(ks[13], (DEPTH, N_GROUPS, SSM_GROUP, SSM_STATE), f32) * SSM_STATE ** -0.5,
        'ssm_c_im': nrm(ks[14], (DEPTH, N_GROUPS, SSM_GROUP, SSM_STATE), f32) * SSM_STATE ** -0.5,
        'ssm_d': nrm(ks[15], (DEPTH, D_SSM), f32),
        'ssm_glu_w': nrm(ks[16], (DEPTH, D_SSM, 2 * D_MODEL), f32) * D_SSM ** -0.5,
        'attn_rel_bias': 0.5 * nrm(ks[17], (DEPTH, N_HEADS, 2 * REL_CLIP + 1), f32),
        'attn_w_o': nrm(ks[18], (DEPTH, D_ATTN, D_MODEL), f32) * D_ATTN ** -0.5,
        'w_out': nrm(ks[19], (DEPTH, D_MODEL, D_MODEL), f32) * D_MODEL ** -0.5,
        'norm2_g': 1.0 + 0.05 * nrm(ks[20], (DEPTH, D_MODEL), f32),
        'peer_w_q': nrm(ks[21], (DEPTH, D_MODEL, PEER_HEADS * PEER_DK), f32) * D_MODEL ** -0.5,
        'peer_sub_keys': nrm(ks[22], (DEPTH, PEER_HEADS, 2, N_KEYS, PEER_HALF), f32) * PEER_HALF ** -0.5,
        'peer_u': nrm(ks[23], (DEPTH, N_EXPERTS, D_MODEL), f32) * D_MODEL ** -0.5,
        'peer_v': nrm(ks[24], (DEPTH, N_EXPERTS, D_MODEL), f32) * PEER_HEADS ** -0.5,
        'final_g': 1.0 + 0.05 * nrm(ks[25], (D_MODEL,), f32),
    }


def reference(x_prompt, x_sample, cache_k, cache_v, state_ssm_re, state_ssm_im, norm1_g, w_in,
              ssm_a_re, ssm_a_im, ssm_log_dt, ssm_b_re, ssm_b_im, ssm_c_re, ssm_c_im, ssm_d, ssm_glu_w,
              attn_rel_bias, attn_w_o, w_out, norm2_g, peer_w_q, peer_sub_keys, peer_u, peer_v, final_g):
    xp, xs = x_prompt, x_sample
    h_zero = jnp.zeros((xp.shape[0], N_GROUPS, SSM_STATE), jnp.float32)
    p_re, p_im, p_k, p_v = [], [], [], []
    s_re, s_im, s_k, s_v = [], [], [], []
    for l in range(DEPTH):
        wl = (norm1_g[l], w_in[l], ssm_a_re[l], ssm_a_im[l], ssm_log_dt[l], ssm_b_re[l], ssm_b_im[l],
              ssm_c_re[l], ssm_c_im[l], ssm_d[l], ssm_glu_w[l], attn_rel_bias[l], attn_w_o[l], w_out[l],
              norm2_g[l], peer_w_q[l], peer_sub_keys[l], peer_u[l], peer_v[l])
        xp, hr, hi, kk, vv = trunk_layer(xp, h_zero, h_zero, None, None, *wl)
        p_re.append(hr); p_im.append(hi); p_k.append(kk); p_v.append(vv)
        xs, hr, hi, kk, vv = trunk_layer(xs, state_ssm_re[l], state_ssm_im[l], cache_k[l], cache_v[l], *wl)
        s_re.append(hr); s_im.append(hi); s_k.append(kk); s_v.append(vv)
    y_prompt = rms_norm(xp, final_g)
    y_sample = rms_norm(xs, final_g)
    return (y_prompt, y_sample,
            jnp.stack(p_re), jnp.stack(p_im), jnp.stack(p_k), jnp.stack(p_v),
            jnp.stack(s_re), jnp.stack(s_im), jnp.stack(s_k), jnp.stack(s_v))
```

```python
import functools
import math

import jax
import jax.numpy as jnp
import numpy as np
from jax import lax
from jax.experimental import pallas as pl
from jax.experimental.pallas import tpu as pltpu

F32 = jnp.float32
BF16 = jnp.bfloat16

D_MODEL = 1024
CHUNK = 64
D_SSM = 512
SSM_GROUP = 16
N_GROUPS = D_SSM // SSM_GROUP
SSM_STATE = 64
N_SSM = N_GROUPS * SSM_STATE
N_HEADS = 8
HEAD_DIM = 64
D_ATTN = N_HEADS * HEAD_DIM
LEFT_CHUNKS = 8
REL_CLIP = 128
ATTN_SCALE = HEAD_DIM ** -0.5
N_KEYS = 128
N_EXPERTS = N_KEYS * N_KEYS
PEER_HEADS = 8
PEER_DK = 256
PEER_HALF = PEER_DK // 2
PEER_TOPK = 16
PEER_SLOTS = PEER_HEADS * PEER_TOPK
RMS_EPS = 1e-6
D_IN = D_SSM + 3 * D_ATTN + 2 * D_MODEL

VMEM_LIMIT_BYTES = 56 * 1024 * 1024
TOKEN_TILE = 256
ROUTE_TILE = 128
PEER_TOKENS = 32
SCAN_CHUNK = 256
NEG_INF = float("-inf")

PEER_CAND = [(i, j) for i in range(PEER_TOPK) for j in range(PEER_TOPK) if (i + 1) * (j + 1) <= PEER_TOPK]
N_CAND_ROWS = 56


def _cparams(n_axes):
    return pltpu.CompilerParams(dimension_semantics=("arbitrary",) * n_axes,
                                vmem_limit_bytes=VMEM_LIMIT_BYTES)


def _const_spec(shape):
    return pl.BlockSpec(shape, lambda *_: (0,) * len(shape))


def _inproj_kernel(x_ref, g_ref, w_ref, u_ref, q_ref, kv_ref, kvb_ref, gate_ref):
    x = x_ref[...]
    ms = jnp.mean(x * x, axis=-1, keepdims=True)
    xn = (x * lax.rsqrt(ms + RMS_EPS) * g_ref[...]).astype(BF16)

    def proj(lo, hi):
        return jnp.dot(xn, w_ref[:, lo:hi], preferred_element_type=F32)

    u_ref[...] = proj(0, D_SSM).astype(BF16)
    q_ref[...] = proj(D_SSM, D_SSM + D_ATTN).astype(BF16)
    for c in range(2):
        lo = D_SSM + D_ATTN + c * D_ATTN
        kv = proj(lo, lo + D_ATTN)
        kv_ref[:, c * D_ATTN:(c + 1) * D_ATTN] = kv
        kvb_ref[:, c * D_ATTN:(c + 1) * D_ATTN] = kv.astype(BF16)
    for c in range(4):
        lo = D_SSM + 3 * D_ATTN + c * 512
        gate_ref[:, c * 512:(c + 1) * 512] = jax.nn.sigmoid(proj(lo, lo + 512)).astype(BF16)


def _inproj(x, g, w_bf):
    t = x.shape[0]
    tm = TOKEN_TILE
    row = lambda n: pl.BlockSpec((tm, n), lambda i: (i, 0))
    return pl.pallas_call(
        _inproj_kernel,
        grid=(t // tm,),
        in_specs=[row(D_MODEL), _const_spec((1, D_MODEL)), _const_spec((D_MODEL, D_IN))],
        out_specs=[row(D_SSM), row(D_ATTN), row(2 * D_ATTN), row(2 * D_ATTN), row(2 * D_MODEL)],
        out_shape=[jax.ShapeDtypeStruct((t, D_SSM), BF16), jax.ShapeDtypeStruct((t, D_ATTN), BF16),
                   jax.ShapeDtypeStruct((t, 2 * D_ATTN), F32), jax.ShapeDtypeStruct((t, 2 * D_ATTN), BF16),
                   jax.ShapeDtypeStruct((t, 2 * D_MODEL), BF16)],
        compiler_params=_cparams(1),
        name="inproj",
    )(x, g, w_bf)


def _ssm_kernel(lc, u_ref, bcat_ref, apr_ref, api_ref, ccat_ref, d_ref, h0r_ref, h0i_ref,
                y_ref, hr_ref, hi_ref, re_ref, im_ref, cr_ref, ci_ref):
    c = pl.program_id(1)

    @pl.when(c == 0)
    def _():
        cr_ref[...] = h0r_ref[0]
        ci_ref[...] = h0i_ref[0]

    u = u_ref[0]
    bu = jnp.dot(u, bcat_ref[...], preferred_element_type=F32)
    re_ref[...] = bu[:, :N_SSM]
    im_ref[...] = bu[:, N_SSM:]

    row = lax.broadcasted_iota(jnp.int32, (lc, 128), 0)

    def lane_tile(j, _):
        sl = pl.ds(pl.multiple_of(j * 128, 128), 128)
        hr = re_ref[:, sl]
        hi = im_ref[:, sl]
        s = 1
        while s < lc:
            ar = apr_ref[s - 1:s, sl]
            ai = api_ref[s - 1:s, sl]
            keep = row >= s
            sr = jnp.where(keep, pltpu.roll(hr, s, 0), 0.0)
            si = jnp.where(keep, pltpu.roll(hi, s, 0), 0.0)
            hr, hi = hr + (ar * sr - ai * si), hi + (ar * si + ai * sr)
            s *= 2
        pr = apr_ref[:, sl]
        pi = api_ref[:, sl]
        c_r = cr_ref[:, sl]
        c_i = ci_ref[:, sl]
        hr = hr + (pr * c_r - pi * c_i)
        hi = hi + (pr * c_i + pi * c_r)
        re_ref[:, sl] = hr
        im_ref[:, sl] = hi
        cr_ref[:, sl] = hr[lc - 1:lc]
        ci_ref[:, sl] = hi[lc - 1:lc]
        return 0

    lax.fori_loop(0, N_SSM // 128, lane_tile, 0)

    hr_ref[0] = cr_ref[...]
    hi_ref[0] = ci_ref[...]
    y = jnp.dot(re_ref[...].astype(BF16), ccat_ref[:N_SSM], preferred_element_type=F32)
    y = y + jnp.dot(im_ref[...].astype(BF16), ccat_ref[N_SSM:], preferred_element_type=F32)
    y_ref[0] = (y + d_ref[...] * u.astype(F32)).astype(BF16)


def _ssm(u, h0r, h0i, bcat, apr, api, ccat, dskip, lc):
    b, l, _ = u.shape
    st = jax.ShapeDtypeStruct((b, 1, N_SSM), F32)
    state_spec = pl.BlockSpec((1, 1, N_SSM), lambda i, c: (i, 0, 0))
    return pl.pallas_call(
        functools.partial(_ssm_kernel, lc),
        grid=(b, l // lc),
        in_specs=[pl.BlockSpec((1, lc, D_SSM), lambda i, c: (i, c, 0)),
                  _const_spec((D_SSM, 2 * N_SSM)), _const_spec((lc, N_SSM)), _const_spec((lc, N_SSM)),
                  _const_spec((2 * N_SSM, D_SSM)), _const_spec((1, D_SSM)), state_spec, state_spec],
        out_specs=[pl.BlockSpec((1, lc, D_SSM), lambda i, c: (i, c, 0)), state_spec, state_spec],
        out_shape=[jax.ShapeDtypeStruct((b, l, D_SSM), BF16), st, st],
        scratch_shapes=[pltpu.VMEM((lc, N_SSM), F32), pltpu.VMEM((lc, N_SSM), F32),
                        pltpu.VMEM((1, N_SSM), F32), pltpu.VMEM((1, N_SSM), F32)],
        compiler_params=_cparams(2),
        name=f"ssm_scan_{lc}",
    )(u, bcat, apr, api, ccat, dskip, h0r.reshape(b, 1, N_SSM), h0i.reshape(b, 1, N_SSM))


def _ssm_params(a_re, a_im, log_dt, b_re, b_im, c_re, c_im, lc):
    a = lax.complex(a_re, a_im)
    adt = a * jnp.exp(log_dt)[:, None]
    a_bar = jnp.exp(adt)
    b_bar = ((a_bar - 1.0) / a)[..., None] * lax.complex(b_re, b_im)
    eye = jnp.eye(N_GROUPS, dtype=F32)
    bre = jnp.einsum("gpc,gh->gchp", jnp.real(b_bar), eye).reshape(D_SSM, N_SSM)
    bim = jnp.einsum("gpc,gh->gchp", jnp.imag(b_bar), eye).reshape(D_SSM, N_SSM)
    bcat = jnp.concatenate([bre, bim], axis=1).astype(BF16)
    cre = jnp.einsum("gcp,gh->gphc", c_re, eye).reshape(N_SSM, D_SSM)
    cim = jnp.einsum("gcp,gh->gphc", c_im, eye).reshape(N_SSM, D_SSM)
    ccat = jnp.concatenate([cre, -cim], axis=0).astype(BF16)
    steps = jnp.arange(1, lc + 1, dtype=F32)[:, None, None]
    apow = jnp.exp(adt[None] * steps).reshape(lc, N_SSM)
    return bcat, jnp.real(apow), jnp.imag(apow), ccat


def _attn_kernel(qc, w, masked, q_ref, k_ref, v_ref, bias_ref, o_ref):
    n = pl.program_id(1)
    q = q_ref[0]
    head = lax.broadcasted_iota(jnp.int32, (qc, D_ATTN), 1) // HEAD_DIM
    qs = jnp.concatenate([jnp.where(head == h, q, jnp.zeros_like(q)) for h in range(N_HEADS)], axis=0)
    start = pl.multiple_of(n * qc, qc)
    k = k_ref[0, pl.ds(start, w), :]
    v = v_ref[0, pl.ds(start, w), :]
    s = lax.dot_general(qs, k, (((1,), (1,)), ((), ())), preferred_element_type=F32)
    s = s * ATTN_SCALE + bias_ref[...]
    if masked:
        col = lax.broadcasted_iota(jnp.int32, (N_HEADS * qc, w), 1)
        s = jnp.where(col + n * qc >= LEFT_CHUNKS * CHUNK, s, jnp.finfo(F32).min)
    m = jnp.max(s, axis=-1, keepdims=True)
    p = jnp.exp(s - m)
    p = p / jnp.sum(p, axis=-1, keepdims=True)
    r = jnp.dot(p.astype(BF16), v, preferred_element_type=F32)
    o = jnp.zeros((qc, D_ATTN), F32)
    for h in range(N_HEADS):
        o = o + jnp.where(head == h, r[h * qc:(h + 1) * qc], 0.0)
    o_ref[0] = o.astype(BF16)


def _attention(q, k, v, bias, qc, w, masked):
    b, l, _ = q.shape
    lk = k.shape[1]
    kv_spec = pl.BlockSpec((1, lk, D_ATTN), lambda i, n: (i, 0, 0))
    return pl.pallas_call(
        functools.partial(_attn_kernel, qc, w, masked),
        grid=(b, l // qc),
        in_specs=[pl.BlockSpec((1, qc, D_ATTN), lambda i, n: (i, n, 0)), kv_spec, kv_spec,
                  _const_spec((N_HEADS * qc, w))],
        out_specs=pl.BlockSpec((1, qc, D_ATTN), lambda i, n: (i, n, 0)),
        out_shape=jax.ShapeDtypeStruct((b, l, D_ATTN), BF16),
        compiler_params=_cparams(2),
        name=f"band_attn_{qc}",
    )(q, k, v, bias)


def _bias_table(rel_bias, qc, w):
    rel = (jnp.arange(w) - LEFT_CHUNKS * CHUNK)[None, :] - jnp.arange(qc)[:, None]
    idx = jnp.clip(rel, -REL_CLIP, REL_CLIP) + REL_CLIP
    return rel_bias[:, idx].astype(F32).reshape(N_HEADS * qc, w)


def _mix_kernel(x_ref, y_ref, att_ref, gate_ref, glu_ref, wo_ref, wout_ref, g2_ref, wq_ref,
                x1_ref, xn_ref, qp_ref):
    glu = jnp.dot(y_ref[...], glu_ref[...], preferred_element_type=F32)
    a = glu[:, :D_MODEL] * jax.nn.sigmoid(glu[:, D_MODEL:])
    b = jnp.dot(att_ref[...], wo_ref[...], preferred_element_type=F32)
    mixed = gate_ref[:, :D_MODEL].astype(F32) * a + gate_ref[:, D_MODEL:].astype(F32) * b
    x1 = x_ref[...] + jnp.dot(mixed.astype(BF16), wout_ref[...], preferred_element_type=F32)
    x1_ref[...] = x1
    ms = jnp.mean(x1 * x1, axis=-1, keepdims=True)
    xn = x1 * lax.rsqrt(ms + RMS_EPS) * g2_ref[...]
    xn_ref[...] = xn
    qp_ref[...] = jnp.dot(xn.astype(BF16), wq_ref[...], preferred_element_type=F32).astype(BF16)


def _mix(x, y, att, gates, glu_w, w_o, w_out, g2, w_q):
    t = x.shape[0]
    tm = TOKEN_TILE
    row = lambda n: pl.BlockSpec((tm, n), lambda i: (i, 0))
    dq = PEER_HEADS * PEER_DK
    return pl.pallas_call(
        _mix_kernel,
        grid=(t // tm,),
        in_specs=[row(D_MODEL), row(D_SSM), row(D_ATTN), row(2 * D_MODEL),
                  _const_spec((D_SSM, 2 * D_MODEL)), _const_spec((D_ATTN, D_MODEL)),
                  _const_spec((D_MODEL, D_MODEL)), _const_spec((1, D_MODEL)), _const_spec((D_MODEL, dq))],
        out_specs=[row(D_MODEL), row(D_MODEL), row(dq)],
        out_shape=[jax.ShapeDtypeStruct((t, D_MODEL), F32), jax.ShapeDtypeStruct((t, D_MODEL), F32),
                   jax.ShapeDtypeStruct((t, dq), BF16)],
        compiler_params=_cparams(1),
        name="mix",
    )(x, y, att, gates, glu_w, w_o, w_out, g2, w_q)


def _extract_max(s, row):
    m = jnp.max(s, axis=0, keepdims=True)
    pos = jnp.min(jnp.where(s == m, row, float(s.shape[0])), axis=0, keepdims=True)
    return m, pos, jnp.where(row == pos, NEG_INF, s)


def _route_kernel(qp_ref, keys_ref, e_ref, g_ref):
    tm = ROUTE_TILE
    row_k = lax.broadcasted_iota(jnp.int32, (N_KEYS, tm), 0).astype(F32)
    row_c = lax.broadcasted_iota(jnp.int32, (N_CAND_ROWS, tm), 0).astype(F32)
    row_o = lax.broadcasted_iota(jnp.int32, (PEER_TOPK, tm), 0)
    for h in range(PEER_HEADS):
        vals, ids = [], []
        for z in range(2):
            hz = 2 * h + z
            q = qp_ref[:, hz * PEER_HALF:(hz + 1) * PEER_HALF]
            s = lax.dot_general(keys_ref[hz], q, (((1,), (1,)), ((), ())),
                                preferred_element_type=F32)
            v_z, i_z = [], []
            for _ in range(PEER_TOPK):
                m, pos, s = _extract_max(s, row_k)
                v_z.append(m)
                i_z.append(pos)
            vals.append(v_z)
            ids.append(i_z)
        cand = jnp.full((N_CAND_ROWS, tm), NEG_INF, F32)
        cand_e = jnp.zeros((N_CAND_ROWS, tm), F32)
        for c, (i, j) in enumerate(PEER_CAND):
            cand = jnp.where(row_c == float(c), vals[0][i] + vals[1][j], cand)
            cand_e = jnp.where(row_c == float(c), ids[0][i] * float(N_KEYS) + ids[1][j], cand_e)
        best = jnp.zeros((PEER_TOPK, tm), F32)
        best_e = jnp.zeros((PEER_TOPK, tm), F32)
        top = None
        for k in range(PEER_TOPK):
            m, pos, cand = _extract_max(cand, row_c)
            e = jnp.max(jnp.where(row_c == pos, cand_e, -1.0), axis=0, keepdims=True)
            top = m if top is None else top
            best = jnp.where(row_o == k, jnp.exp(m - top), best)
            best_e = jnp.where(row_o == k, e, best_e)
        e_ref[h * PEER_TOPK:(h + 1) * PEER_TOPK, :] = best_e.astype(jnp.int32)
        g_ref[h * PEER_TOPK:(h + 1) * PEER_TOPK, :] = best / jnp.sum(best, axis=0, keepdims=True)


def _route(qp, keys_bf):
    t = qp.shape[0]
    tm = ROUTE_TILE
    out = pl.BlockSpec((PEER_SLOTS, tm), lambda i: (0, i))
    return pl.pallas_call(
        _route_kernel,
        grid=(t // tm,),
        in_specs=[pl.BlockSpec((tm, PEER_HEADS * PEER_DK), lambda i: (i, 0)),
                  _const_spec((2 * PEER_HEADS, N_KEYS, PEER_HALF))],
        out_specs=[out, out],
        out_shape=[jax.ShapeDtypeStruct((PEER_SLOTS, t), jnp.int32),
                   jax.ShapeDtypeStruct((PEER_SLOTS, t), F32)],
        compiler_params=_cparams(1),
        name="peer_route",
    )(qp, keys_bf)


def _pack_table(tab):
    tb = tab.astype(BF16)
    hi = lax.bitcast_convert_type(tb[:, :512], jnp.uint16).astype(jnp.uint32)
    lo = lax.bitcast_convert_type(tb[:, 512:], jnp.uint16).astype(jnp.uint32)
    return ((hi << 16) | lo).reshape(N_EXPERTS, 4, 128)


def _unpack(word):
    hi = pltpu.bitcast(word & jnp.uint32(0xFFFF0000), F32)
    lo = pltpu.bitcast(word << 16, F32)
    return hi, lo


def _split_bf16(x):
    hi = x.astype(BF16)
    lo = (x - hi.astype(F32)).astype(BF16)
    return hi, lo


def _peer_u_kernel(idx_ref, tab_ref, x_ref, g_ref, w_ref, s_ref):
    ones = jnp.ones((8, 128), BF16)

    def token(t, _):
        base = t * PEER_SLOTS
        xt = x_ref[t]
        x_hi = xt[0:4]
        x_lo = xt[4:8]
        for k in range(PEER_SLOTS):
            hi, lo = _unpack(tab_ref[idx_ref[base + k]])
            p = hi * x_hi + lo * x_lo
            p = p[0:2] + p[2:4]
            s_ref[k:k + 1, :] = p[0:1] + p[1:2]
        s_hi, s_lo = _split_bf16(s_ref[...])
        nt = (((1,), (1,)), ((), ()))
        act = (lax.dot_general(ones, s_hi, nt, preferred_element_type=F32)
               + lax.dot_general(ones, s_lo, nt, preferred_element_type=F32))[0:1]
        gelu = 0.5 * act * (1.0 + lax.erf(act * (1.0 / math.sqrt(2.0))))
        w_ref[pl.ds(t, 1), :] = g_ref[pl.ds(t, 1), :] * gelu
        return 0

    lax.fori_loop(0, PEER_TOKENS, token, 0)


def _peer_v_kernel(idx_ref, wgt_ref, tab_ref, x_ref, out_ref):
    def token(t, _):
        base = t * PEER_SLOTS
        acc_hi = [jnp.zeros((4, 128), F32) for _ in range(2)]
        acc_lo = [jnp.zeros((4, 128), F32) for _ in range(2)]
        for k in range(PEER_SLOTS):
            hi, lo = _unpack(tab_ref[idx_ref[base + k]])
            w = wgt_ref[base + k]
            acc_hi[k % 2] = acc_hi[k % 2] + w * hi
            acc_lo[k % 2] = acc_lo[k % 2] + w * lo
        xt = x_ref[t]
        out_ref[t, 0:4, :] = xt[0:4] + (acc_hi[0] + acc_hi[1])
        out_ref[t, 4:8, :] = xt[4:8] + (acc_lo[0] + acc_lo[1])
        return 0

    lax.fori_loop(0, PEER_TOKENS, token, 0)


def _smem_tokens_spec():
    return pl.BlockSpec((PEER_TOKENS * PEER_SLOTS,), lambda i: (i,), memory_space=pltpu.SMEM)


def _table_spec():
    return pl.BlockSpec((N_EXPERTS, 4, 128), lambda i: (0, 0, 0), pipeline_mode=pl.Buffered(1))


def _peer_u(idx_flat, tab, x3, g):
    t = x3.shape[0]
    tok3 = pl.BlockSpec((PEER_TOKENS, 8, 128), lambda i: (i, 0, 0))
    tok2 = pl.BlockSpec((PEER_TOKENS, PEER_SLOTS), lambda i: (i, 0))
    return pl.pallas_call(
        _peer_u_kernel,
        grid=(t // PEER_TOKENS,),
        in_specs=[_smem_tokens_spec(), _table_spec(), tok3, tok2],
        out_specs=tok2,
        out_shape=jax.ShapeDtypeStruct((t, PEER_SLOTS), F32),
        scratch_shapes=[pltpu.VMEM((PEER_SLOTS, 128), F32)],
        compiler_params=_cparams(1),
        name="peer_u",
    )(idx_flat, tab, x3, g)


def _peer_v(idx_flat, w_flat, tab, x3):
    t = x3.shape[0]
    tok3 = pl.BlockSpec((PEER_TOKENS, 8, 128), lambda i: (i, 0, 0))
    return pl.pallas_call(
        _peer_v_kernel,
        grid=(t // PEER_TOKENS,),
        in_specs=[_smem_tokens_spec(), _smem_tokens_spec(), _table_spec(), tok3],
        out_specs=tok3,
        out_shape=jax.ShapeDtypeStruct((t, 8, 128), F32),
        compiler_params=_cparams(1),
        name="peer_v",
    )(idx_flat, w_flat, tab, x3)


def _final_norm_kernel(x_ref, g_ref, y_ref):
    x = x_ref[...]
    ms = jnp.mean(x * x, axis=-1, keepdims=True)
    y_ref[...] = x * lax.rsqrt(ms + RMS_EPS) * g_ref[...]


def _final_norm(x, g):
    t = x.shape[0]
    tm = TOKEN_TILE
    return pl.pallas_call(
        _final_norm_kernel,
        grid=(t // tm,),
        in_specs=[pl.BlockSpec((tm, D_MODEL), lambda i: (i, 0)), _const_spec((1, D_MODEL))],
        out_specs=pl.BlockSpec((tm, D_MODEL), lambda i: (i, 0)),
        out_shape=jax.ShapeDtypeStruct((t, D_MODEL), F32),
        compiler_params=_cparams(1),
        name="final_norm",
    )(x, g)


def kernel(x_prompt, x_sample, cache_k, cache_v, state_ssm_re, state_ssm_im, norm1_g, w_in, ssm_a_re, ssm_a_im, ssm_log_dt, ssm_b_re, ssm_b_im, ssm_c_re, ssm_c_im, ssm_d, ssm_glu_w, attn_rel_bias, attn_w_o, w_out, norm2_g, peer_w_q, peer_sub_keys, peer_u, peer_v, final_g):
    bp, lp, _ = x_prompt.shape
    bs, ls, _ = x_sample.shape
    depth = w_in.shape[0]
    tp = bp * lp
    kv_win = cache_k.shape[2]
    keep = min(LEFT_CHUNKS * CHUNK, lp)
    x = jnp.concatenate([x_prompt.reshape(tp, D_MODEL), x_sample.reshape(bs * ls, D_MODEL)], axis=0)
    t = x.shape[0]
    assert t % TOKEN_TILE == 0 and lp % SCAN_CHUNK == 0 and lp % CHUNK == 0 and kv_win == LEFT_CHUNKS * CHUNK

    zeros_state = jnp.zeros((bp, N_SSM), F32)
    outs = {n: [] for n in ("p_re", "p_im", "p_k", "p_v", "s_re", "s_im", "s_k", "s_v")}
    for l in range(depth):
        u, q, kv, kvb, gates = _inproj(x, norm1_g[l][None], w_in[l].astype(BF16))

        ssm_args = (ssm_a_re[l], ssm_a_im[l], ssm_log_dt[l], ssm_b_re[l], ssm_b_im[l], ssm_c_re[l], ssm_c_im[l])
        dskip = ssm_d[l][None]
        bcat, apr, api, ccat = _ssm_params(*ssm_args, SCAN_CHUNK)
        y_p, hr_p, hi_p = _ssm(u[:tp].reshape(bp, lp, D_SSM), zeros_state, zeros_state,
                               bcat, apr, api, ccat, dskip, SCAN_CHUNK)
        y_s, hr_s, hi_s = _ssm(u[tp:].reshape(bs, ls, D_SSM), state_ssm_re[l].reshape(bs, N_SSM),
                               state_ssm_im[l].reshape(bs, N_SSM), bcat, apr[:ls], api[:ls], ccat, dskip, ls)
        y = jnp.concatenate([y_p.reshape(tp, D_SSM), y_s.reshape(bs * ls, D_SSM)], axis=0)

        pad = ((0, 0), (LEFT_CHUNKS * CHUNK, 0), (0, 0))
        kb_p = jnp.pad(kvb[:tp, :D_ATTN].reshape(bp, lp, D_ATTN), pad)
        vb_p = jnp.pad(kvb[:tp, D_ATTN:].reshape(bp, lp, D_ATTN), pad)
        w_p = (LEFT_CHUNKS + 1) * CHUNK
        att_p = _attention(q[:tp].reshape(bp, lp, D_ATTN), kb_p, vb_p,
                           _bias_table(attn_rel_bias[l], CHUNK, w_p), CHUNK, w_p, True)
        kb_s = jnp.concatenate([cache_k[l].reshape(bs, kv_win, D_ATTN).astype(BF16),
                                kvb[tp:, :D_ATTN].reshape(bs, ls, D_ATTN)], axis=1)
        vb_s = jnp.concatenate([cache_v[l].reshape(bs, kv_win, D_ATTN).astype(BF16),
                                kvb[tp:, D_ATTN:].reshape(bs, ls, D_ATTN)], axis=1)
        w_s = kv_win + ls
        att_s = _attention(q[tp:].reshape(bs, ls, D_ATTN), kb_s, vb_s,
                           _bias_table(attn_rel_bias[l], ls, w_s), ls, w_s, False)
        att = jnp.concatenate([att_p.reshape(tp, D_ATTN), att_s.reshape(bs * ls, D_ATTN)], axis=0)

        x1, xn, qp = _mix(x, y, att, gates, ssm_glu_w[l].astype(BF16), attn_w_o[l].astype(BF16),
                          w_out[l].astype(BF16), norm2_g[l][None], peer_w_q[l].astype(BF16))
        keys = peer_sub_keys[l].reshape(2 * PEER_HEADS, N_KEYS, PEER_HALF).astype(BF16)
        e_t, g_t = _route(qp, keys)
        idx_flat = e_t.T.reshape(t * PEER_SLOTS)
        wgt = _peer_u(idx_flat, _pack_table(peer_u[l]), xn.reshape(t, 8, 128), g_t.T)
        x = _peer_v(idx_flat, wgt.reshape(t * PEER_SLOTS), _pack_table(peer_v[l]),
                    x1.reshape(t, 8, 128)).reshape(t, D_MODEL)

        kv_p = kv[:tp].reshape(bp, lp, 2, N_HEADS, HEAD_DIM)[:, lp - keep:]
        kv_s = kv[tp:].reshape(bs, ls, 2, N_HEADS, HEAD_DIM)
        outs["p_re"].append(hr_p.reshape(bp, N_GROUPS, SSM_STATE))
        outs["p_im"].append(hi_p.reshape(bp, N_GROUPS, SSM_STATE))
        outs["p_k"].append(kv_p[:, :, 0])
        outs["p_v"].append(kv_p[:, :, 1])
        outs["s_re"].append(hr_s.reshape(bs, N_GROUPS, SSM_STATE))
        outs["s_im"].append(hi_s.reshape(bs, N_GROUPS, SSM_STATE))
        outs["s_k"].append(kv_s[:, :, 0])
        outs["s_v"].append(kv_s[:, :, 1])

    y = _final_norm(x, final_g[None])
    st = {n: jnp.stack(v) for n, v in outs.items()}
    return (y[:tp].reshape(bp, lp, D_MODEL), y[tp:].reshape(bs, ls, D_MODEL),
            st["p_re"], st["p_im"], st["p_k"], st["p_v"], st["s_re"], st["s_im"], st["s_k"], st["s_v"])
```

```python
import functools
import math

import jax
import jax.numpy as jnp
import numpy as np
from jax import lax
from jax.experimental import pallas as pl
from jax.experimental.pallas import tpu as pltpu

F32 = jnp.float32
BF16 = jnp.bfloat16

D_MODEL = 1024
CHUNK = 64
D_SSM = 512
SSM_GROUP = 16
N_GROUPS = D_SSM // SSM_GROUP
SSM_STATE = 64
N_SSM = N_GROUPS * SSM_STATE
N_HEADS = 8
HEAD_DIM = 64
D_ATTN = N_HEADS * HEAD_DIM
LEFT_CHUNKS = 8
REL_CLIP = 128
ATTN_SCALE = HEAD_DIM ** -0.5
N_KEYS = 128
N_EXPERTS = N_KEYS * N_KEYS
PEER_HEADS = 8
PEER_DK = 256
PEER_HALF = PEER_DK // 2
PEER_TOPK = 16
PEER_SLOTS = PEER_HEADS * PEER_TOPK
RMS_EPS = 1e-6
D_IN = D_SSM + 3 * D_ATTN + 2 * D_MODEL

VMEM_LIMIT_BYTES = 56 * 1024 * 1024
TOKEN_TILE = 256
ROUTE_TILE = 128
PEER_TOKENS = 128
PEER_UNROLL = 8
SCAN_CHUNK = 256
NEG_INF = float("-inf")

ROW_WORDS = 4
G_STRIDE = 136
G_ROWS = ROW_WORDS * G_STRIDE

PEER_CAND = [(i, j) for i in range(PEER_TOPK) for j in range(PEER_TOPK) if (i + 1) * (j + 1) <= PEER_TOPK]
N_CAND_ROWS = 56


def _cparams(n_axes):
    return pltpu.CompilerParams(dimension_semantics=("arbitrary",) * n_axes,
                                vmem_limit_bytes=VMEM_LIMIT_BYTES)


def _const_spec(shape):
    return pl.BlockSpec(shape, lambda *_: (0,) * len(shape))


def _inproj_kernel(x_ref, g_ref, w_ref, u_ref, q_ref, kv_ref, kvb_ref, gate_ref):
    x = x_ref[...]
    ms = jnp.mean(x * x, axis=-1, keepdims=True)
    xn = (x * lax.rsqrt(ms + RMS_EPS) * g_ref[...]).astype(BF16)

    def proj(lo, hi):
        return jnp.dot(xn, w_ref[:, lo:hi], preferred_element_type=F32)

    u_ref[...] = proj(0, D_SSM).astype(BF16)
    q_ref[...] = proj(D_SSM, D_SSM + D_ATTN).astype(BF16)
    for c in range(2):
        lo = D_SSM + D_ATTN + c * D_ATTN
        kv = proj(lo, lo + D_ATTN)
        kv_ref[:, c * D_ATTN:(c + 1) * D_ATTN] = kv
        kvb_ref[:, c * D_ATTN:(c + 1) * D_ATTN] = kv.astype(BF16)
    for c in range(4):
        lo = D_SSM + 3 * D_ATTN + c * 512
        gate_ref[:, c * 512:(c + 1) * 512] = jax.nn.sigmoid(proj(lo, lo + 512)).astype(BF16)


def _inproj(x, g, w_bf):
    t = x.shape[0]
    tm = TOKEN_TILE
    row = lambda n: pl.BlockSpec((tm, n), lambda i: (i, 0))
    return pl.pallas_call(
        _inproj_kernel,
        grid=(t // tm,),
        in_specs=[row(D_MODEL), _const_spec((1, D_MODEL)), _const_spec((D_MODEL, D_IN))],
        out_specs=[row(D_SSM), row(D_ATTN), row(2 * D_ATTN), row(2 * D_ATTN), row(2 * D_MODEL)],
        out_shape=[jax.ShapeDtypeStruct((t, D_SSM), BF16), jax.ShapeDtypeStruct((t, D_ATTN), BF16),
                   jax.ShapeDtypeStruct((t, 2 * D_ATTN), F32), jax.ShapeDtypeStruct((t, 2 * D_ATTN), BF16),
                   jax.ShapeDtypeStruct((t, 2 * D_MODEL), BF16)],
        compiler_params=_cparams(1),
        name="inproj",
    )(x, g, w_bf)


def _ssm_kernel(lc, u_ref, bcat_ref, apr_ref, api_ref, ccat_ref, d_ref, h0r_ref, h0i_ref,
                y_ref, hr_ref, hi_ref, re_ref, im_ref, cr_ref, ci_ref):
    c = pl.program_id(1)

    @pl.when(c == 0)
    def _():
        cr_ref[...] = h0r_ref[0]
        ci_ref[...] = h0i_ref[0]

    u = u_ref[0]
    bu = jnp.dot(u, bcat_ref[...], preferred_element_type=F32)
    re_ref[...] = bu[:, :N_SSM]
    im_ref[...] = bu[:, N_SSM:]

    row = lax.broadcasted_iota(jnp.int32, (lc, 128), 0)

    def lane_tile(j, _):
        sl = pl.ds(pl.multiple_of(j * 128, 128), 128)
        hr = re_ref[:, sl]
        hi = im_ref[:, sl]
        s = 1
        while s < lc:
            ar = apr_ref[s - 1:s, sl]
            ai = api_ref[s - 1:s, sl]
            keep = row >= s
            sr = jnp.where(keep, pltpu.roll(hr, s, 0), 0.0)
            si = jnp.where(keep, pltpu.roll(hi, s, 0), 0.0)
            hr, hi = hr + (ar * sr - ai * si), hi + (ar * si + ai * sr)
            s *= 2
        pr = apr_ref[:, sl]
        pi = api_ref[:, sl]
        c_r = cr_ref[:, sl]
        c_i = ci_ref[:, sl]
        hr = hr + (pr * c_r - pi * c_i)
        hi = hi + (pr * c_i + pi * c_r)
        re_ref[:, sl] = hr
        im_ref[:, sl] = hi
        cr_ref[:, sl] = hr[lc - 1:lc]
        ci_ref[:, sl] = hi[lc - 1:lc]
        return 0

    lax.fori_loop(0, N_SSM // 128, lane_tile, 0)

    hr_ref[0] = cr_ref[...]
    hi_ref[0] = ci_ref[...]
    y = jnp.dot(re_ref[...].astype(BF16), ccat_ref[:N_SSM], preferred_element_type=F32)
    y = y + jnp.dot(im_ref[...].astype(BF16), ccat_ref[N_SSM:], preferred_element_type=F32)
    y_ref[0] = (y + d_ref[...] * u.astype(F32)).astype(BF16)


def _ssm(u, h0r, h0i, bcat, apr, api, ccat, dskip, lc):
    b, l, _ = u.shape
    st = jax.ShapeDtypeStruct((b, 1, N_SSM), F32)
    state_spec = pl.BlockSpec((1, 1, N_SSM), lambda i, c: (i, 0, 0))
    return pl.pallas_call(
        functools.partial(_ssm_kernel, lc),
        grid=(b, l // lc),
        in_specs=[pl.BlockSpec((1, lc, D_SSM), lambda i, c: (i, c, 0)),
                  _const_spec((D_SSM, 2 * N_SSM)), _const_spec((lc, N_SSM)), _const_spec((lc, N_SSM)),
                  _const_spec((2 * N_SSM, D_SSM)), _const_spec((1, D_SSM)), state_spec, state_spec],
        out_specs=[pl.BlockSpec((1, lc, D_SSM), lambda i, c: (i, c, 0)), state_spec, state_spec],
        out_shape=[jax.ShapeDtypeStruct((b, l, D_SSM), BF16), st, st],
        scratch_shapes=[pltpu.VMEM((lc, N_SSM), F32), pltpu.VMEM((lc, N_SSM), F32),
                        pltpu.VMEM((1, N_SSM), F32), pltpu.VMEM((1, N_SSM), F32)],
        compiler_params=_cparams(2),
        name=f"ssm_scan_{lc}",
    )(u, bcat, apr, api, ccat, dskip, h0r.reshape(b, 1, N_SSM), h0i.reshape(b, 1, N_SSM))


def _ssm_params(a_re, a_im, log_dt, b_re, b_im, c_re, c_im, lc):
    a = lax.complex(a_re, a_im)
    adt = a * jnp.exp(log_dt)[:, None]
    a_bar = jnp.exp(adt)
    b_bar = ((a_bar - 1.0) / a)[..., None] * lax.complex(b_re, b_im)
    eye = jnp.eye(N_GROUPS, dtype=F32)
    bre = jnp.einsum("gpc,gh->gchp", jnp.real(b_bar), eye).reshape(D_SSM, N_SSM)
    bim = jnp.einsum("gpc,gh->gchp", jnp.imag(b_bar), eye).reshape(D_SSM, N_SSM)
    bcat = jnp.concatenate([bre, bim], axis=1).astype(BF16)
    cre = jnp.einsum("gcp,gh->gphc", c_re, eye).reshape(N_SSM, D_SSM)
    cim = jnp.einsum("gcp,gh->gphc", c_im, eye).reshape(N_SSM, D_SSM)
    ccat = jnp.concatenate([cre, -cim], axis=0).astype(BF16)
    steps = jnp.arange(1, lc + 1, dtype=F32)[:, None, None]
    apow = jnp.exp(adt[None] * steps).reshape(lc, N_SSM)
    return bcat, jnp.real(apow), jnp.imag(apow), ccat


def _attn_kernel(qc, w, masked, q_ref, k_ref, v_ref, bias_ref, o_ref):
    n = pl.program_id(1)
    q = q_ref[0]
    head = lax.broadcasted_iota(jnp.int32, (qc, D_ATTN), 1) // HEAD_DIM
    qs = jnp.concatenate([jnp.where(head == h, q, jnp.zeros_like(q)) for h in range(N_HEADS)], axis=0)
    start = pl.multiple_of(n * qc, qc)
    k = k_ref[0, pl.ds(start, w), :]
    v = v_ref[0, pl.ds(start, w), :]
    s = lax.dot_general(qs, k, (((1,), (1,)), ((), ())), preferred_element_type=F32)
    s = s * ATTN_SCALE + bias_ref[...]
    if masked:
        col = lax.broadcasted_iota(jnp.int32, (N_HEADS * qc, w), 1)
        s = jnp.where(col + n * qc >= LEFT_CHUNKS * CHUNK, s, jnp.finfo(F32).min)
    m = jnp.max(s, axis=-1, keepdims=True)
    p = jnp.exp(s - m)
    p = p / jnp.sum(p, axis=-1, keepdims=True)
    r = jnp.dot(p.astype(BF16), v, preferred_element_type=F32)
    o = jnp.zeros((qc, D_ATTN), F32)
    for h in range(N_HEADS):
        o = o + jnp.where(head == h, r[h * qc:(h + 1) * qc], 0.0)
    o_ref[0] = o.astype(BF16)


def _attention(q, k, v, bias, qc, w, masked):
    b, l, _ = q.shape
    lk = k.shape[1]
    kv_spec = pl.BlockSpec((1, lk, D_ATTN), lambda i, n: (i, 0, 0))
    return pl.pallas_call(
        functools.partial(_attn_kernel, qc, w, masked),
        grid=(b, l // qc),
        in_specs=[pl.BlockSpec((1, qc, D_ATTN), lambda i, n: (i, n, 0)), kv_spec, kv_spec,
                  _const_spec((N_HEADS * qc, w))],
        out_specs=pl.BlockSpec((1, qc, D_ATTN), lambda i, n: (i, n, 0)),
        out_shape=jax.ShapeDtypeStruct((b, l, D_ATTN), BF16),
        compiler_params=_cparams(2),
        name=f"band_attn_{qc}",
    )(q, k, v, bias)


def _bias_table(rel_bias, qc, w):
    rel = (jnp.arange(w) - LEFT_CHUNKS * CHUNK)[None, :] - jnp.arange(qc)[:, None]
    idx = jnp.clip(rel, -REL_CLIP, REL_CLIP) + REL_CLIP
    return rel_bias[:, idx].astype(F32).reshape(N_HEADS * qc, w)


def _mix_kernel(x_ref, y_ref, att_ref, gate_ref, glu_ref, wo_ref, wout_ref, g2_ref, wq_ref,
                x1_ref, xn_ref, qp_ref):
    glu = jnp.dot(y_ref[...], glu_ref[...], preferred_element_type=F32)
    a = glu[:, :D_MODEL] * jax.nn.sigmoid(glu[:, D_MODEL:])
    b = jnp.dot(att_ref[...], wo_ref[...], preferred_element_type=F32)
    mixed = gate_ref[:, :D_MODEL].astype(F32) * a + gate_ref[:, D_MODEL:].astype(F32) * b
    x1 = x_ref[...] + jnp.dot(mixed.astype(BF16), wout_ref[...], preferred_element_type=F32)
    x1_ref[...] = x1
    ms = jnp.mean(x1 * x1, axis=-1, keepdims=True)
    xn = (x1 * lax.rsqrt(ms + RMS_EPS) * g2_ref[...]).astype(BF16)
    xn_ref[...] = xn
    qp_ref[...] = jnp.dot(xn, wq_ref[...], preferred_element_type=F32).astype(BF16)


def _mix(x, y, att, gates, glu_w, w_o, w_out, g2, w_q):
    t = x.shape[0]
    tm = TOKEN_TILE
    row = lambda n: pl.BlockSpec((tm, n), lambda i: (i, 0))
    dq = PEER_HEADS * PEER_DK
    return pl.pallas_call(
        _mix_kernel,
        grid=(t // tm,),
        in_specs=[row(D_MODEL), row(D_SSM), row(D_ATTN), row(2 * D_MODEL),
                  _const_spec((D_SSM, 2 * D_MODEL)), _const_spec((D_ATTN, D_MODEL)),
                  _const_spec((D_MODEL, D_MODEL)), _const_spec((1, D_MODEL)), _const_spec((D_MODEL, dq))],
        out_specs=[row(D_MODEL), row(D_MODEL), row(dq)],
        out_shape=[jax.ShapeDtypeStruct((t, D_MODEL), F32), jax.ShapeDtypeStruct((t, D_MODEL), BF16),
                   jax.ShapeDtypeStruct((t, dq), BF16)],
        compiler_params=_cparams(1),
        name="mix",
    )(x, y, att, gates, glu_w, w_o, w_out, g2, w_q)


def _extract_max(s, row):
    m = jnp.max(s, axis=0, keepdims=True)
    pos = jnp.min(jnp.where(s == m, row, float(s.shape[0])), axis=0, keepdims=True)
    return m, pos, jnp.where(row == pos, NEG_INF, s)


def _route_kernel(qp_ref, keys_ref, e_ref, g_ref):
    tm = ROUTE_TILE
    row_k = lax.broadcasted_iota(jnp.int32, (N_KEYS, tm), 0).astype(F32)
    row_c = lax.broadcasted_iota(jnp.int32, (N_CAND_ROWS, tm), 0).astype(F32)
    row_o = lax.broadcasted_iota(jnp.int32, (PEER_TOPK, tm), 0)
    for h in range(PEER_HEADS):
        vals, ids = [], []
        for z in range(2):
            hz = 2 * h + z
            q = qp_ref[:, hz * PEER_HALF:(hz + 1) * PEER_HALF]
            s = lax.dot_general(keys_ref[hz], q, (((1,), (1,)), ((), ())),
                                preferred_element_type=F32)
            v_z, i_z = [], []
            for _ in range(PEER_TOPK):
                m, pos, s = _extract_max(s, row_k)
                v_z.append(m)
                i_z.append(pos)
            vals.append(v_z)
            ids.append(i_z)
        cand = jnp.full((N_CAND_ROWS, tm), NEG_INF, F32)
        cand_e = jnp.zeros((N_CAND_ROWS, tm), F32)
        for c, (i, j) in enumerate(PEER_CAND):
            cand = jnp.where(row_c == float(c), vals[0][i] + vals[1][j], cand)
            cand_e = jnp.where(row_c == float(c), ids[0][i] * float(N_KEYS) + ids[1][j], cand_e)
        best = jnp.zeros((PEER_TOPK, tm), F32)
        best_e = jnp.zeros((PEER_TOPK, tm), F32)
        top = None
        for k in range(PEER_TOPK):
            m, pos, cand = _extract_max(cand, row_c)
            e = jnp.max(jnp.where(row_c == pos, cand_e, -1.0), axis=0, keepdims=True)
            top = m if top is None else top
            best = jnp.where(row_o == k, jnp.exp(m - top), best)
            best_e = jnp.where(row_o == k, e, best_e)
        e_ref[h * PEER_TOPK:(h + 1) * PEER_TOPK, :] = best_e.astype(jnp.int32) * ROW_WORDS
        g_ref[h * PEER_TOPK:(h + 1) * PEER_TOPK, :] = best / jnp.sum(best, axis=0, keepdims=True)


def _route(qp, keys_bf):
    t = qp.shape[0]
    tm = ROUTE_TILE
    out = pl.BlockSpec((PEER_SLOTS, tm), lambda i: (0, i))
    return pl.pallas_call(
        _route_kernel,
        grid=(t // tm,),
        in_specs=[pl.BlockSpec((tm, PEER_HEADS * PEER_DK), lambda i: (i, 0)),
                  _const_spec((2 * PEER_HEADS, N_KEYS, PEER_HALF))],
        out_specs=[out, out],
        out_shape=[jax.ShapeDtypeStruct((PEER_SLOTS, t), jnp.int32),
                   jax.ShapeDtypeStruct((PEER_SLOTS, t), F32)],
        compiler_params=_cparams(1),
        name="peer_route",
    )(qp, keys_bf)


def _pack_table(tab):
    tb = tab.astype(BF16)
    hi = lax.bitcast_convert_type(tb[:, :512], jnp.uint16).astype(jnp.uint32)
    lo = lax.bitcast_convert_type(tb[:, 512:], jnp.uint16).astype(jnp.uint32)
    return ((hi << 16) | lo).reshape(N_EXPERTS * ROW_WORDS, 128)


def _gather_token(idx_ref, t, tab_ref, buf_ref):
    tok = idx_ref.at[t]
    for k in range(PEER_SLOTS):
        row = pl.multiple_of(tok[k], ROW_WORDS)
        buf_ref[pl.ds(k, ROW_WORDS, stride=G_STRIDE), :] = tab_ref[pl.ds(row, ROW_WORDS), :]


def _lane_tile(buf_ref, j):
    word = buf_ref[j * G_STRIDE:j * G_STRIDE + PEER_SLOTS, :]
    hi = pltpu.bitcast(word & jnp.uint32(0xFFFF0000), F32)
    lo = pltpu.bitcast(word << 16, F32)
    return hi, lo


def _token_pipeline(idx_ref, tab_ref, bufs, compute):
    last = PEER_TOKENS - 1
    _gather_token(idx_ref, 0, tab_ref, bufs[0])

    def body(i, _):
        for u in range(PEER_UNROLL):
            t = i * PEER_UNROLL + u
            _gather_token(idx_ref, jnp.minimum(t + 1, last), tab_ref, bufs[(u + 1) % 2])
            compute(t, bufs[u % 2])
        return 0

    lax.fori_loop(0, PEER_TOKENS // PEER_UNROLL, body, 0)


def _peer_u_kernel(idx_ref, tab_ref, xt_ref, g_ref, w_ref, buf_a, buf_b, act_ref):
    lane = lax.broadcasted_iota(jnp.int32, (PEER_SLOTS, PEER_TOKENS), 1)
    xw = [jnp.concatenate([xt_ref[128 * j:128 * (j + 1), :], xt_ref[512 + 128 * j:512 + 128 * (j + 1), :]],
                          axis=0).astype(F32) for j in range(ROW_WORDS)]

    def compute(t, buf):
        r = None
        for j in range(ROW_WORDS):
            hi, lo = _lane_tile(buf, j)
            d = jnp.dot(jnp.concatenate([hi, lo], axis=1), xw[j], preferred_element_type=F32)
            r = d if r is None else r + d
        pltpu.store(act_ref, r, mask=lane == t)

    _token_pipeline(idx_ref, tab_ref, (buf_a, buf_b), compute)
    act = act_ref[...]
    gelu = 0.5 * act * (1.0 + lax.erf(act * (1.0 / math.sqrt(2.0))))
    w_ref[...] = g_ref[...] * gelu


def _peer_v_kernel(idx_ref, tab_ref, wgt_ref, x_ref, o_ref, buf_a, buf_b):
    sub = lax.broadcasted_iota(jnp.int32, (8, PEER_SLOTS), 0)

    def compute(t, buf):
        w = wgt_ref[pl.ds(t, 1), :]
        w0 = w.astype(BF16).astype(F32)
        r1 = w - w0
        w1 = r1.astype(BF16).astype(F32)
        w2 = r1 - w1
        lhs = jnp.where(sub == 0, w0, jnp.where(sub == 1, w1, jnp.where(sub == 2, w2, 0.0)))
        tiles = [_lane_tile(buf, j) for j in range(ROW_WORDS)]
        for half in range(2):
            rhs = jnp.concatenate([tl[half] for tl in tiles], axis=1)
            r = jnp.dot(lhs, rhs, preferred_element_type=F32)
            sl = slice(512 * half, 512 * (half + 1))
            o_ref[pl.ds(t, 1), sl] = x_ref[pl.ds(t, 1), sl] + (r[0:1] + r[1:2] + r[2:3])

    _token_pipeline(idx_ref, tab_ref, (buf_a, buf_b), compute)


def _idx_spec():
    return pl.BlockSpec((PEER_TOKENS, PEER_SLOTS), lambda i: (i, 0), memory_space=pltpu.SMEM)


def _table_spec():
    return pl.BlockSpec((N_EXPERTS * ROW_WORDS, 128), lambda i: (0, 0), pipeline_mode=pl.Buffered(1))


def _gather_buffers():
    return [pltpu.VMEM((G_ROWS, 128), jnp.uint32), pltpu.VMEM((G_ROWS, 128), jnp.uint32)]


def _peer_u(idx, tab, xt, g_t):
    t = xt.shape[1]
    col = pl.BlockSpec((PEER_SLOTS, PEER_TOKENS), lambda i: (0, i))
    return pl.pallas_call(
        _peer_u_kernel,
        grid=(t // PEER_TOKENS,),
        in_specs=[_idx_spec(), _table_spec(), pl.BlockSpec((D_MODEL, PEER_TOKENS), lambda i: (0, i)), col],
        out_specs=col,
        out_shape=jax.ShapeDtypeStruct((PEER_SLOTS, t), F32),
        scratch_shapes=_gather_buffers() + [pltpu.VMEM((PEER_SLOTS, PEER_TOKENS), F32)],
        compiler_params=_cparams(1),
        name="peer_u",
    )(idx, tab, xt, g_t)


def _peer_v(idx, tab, wgt, x):
    t = x.shape[0]
    row = pl.BlockSpec((PEER_TOKENS, D_MODEL), lambda i: (i, 0))
    return pl.pallas_call(
        _peer_v_kernel,
        grid=(t // PEER_TOKENS,),
        in_specs=[_idx_spec(), _table_spec(), pl.BlockSpec((PEER_TOKENS, PEER_SLOTS), lambda i: (i, 0)), row],
        out_specs=row,
        out_shape=jax.ShapeDtypeStruct((t, D_MODEL), F32),
        scratch_shapes=_gather_buffers(),
        compiler_params=_cparams(1),
        name="peer_v",
    )(idx, tab, wgt, x)


def _final_norm_kernel(x_ref, g_ref, y_ref):
    x = x_ref[...]
    ms = jnp.mean(x * x, axis=-1, keepdims=True)
    y_ref[...] = x * lax.rsqrt(ms + RMS_EPS) * g_ref[...]


def _final_norm(x, g):
    t = x.shape[0]
    tm = TOKEN_TILE
    return pl.pallas_call(
        _final_norm_kernel,
        grid=(t // tm,),
        in_specs=[pl.BlockSpec((tm, D_MODEL), lambda i: (i, 0)), _const_spec((1, D_MODEL))],
        out_specs=pl.BlockSpec((tm, D_MODEL), lambda i: (i, 0)),
        out_shape=jax.ShapeDtypeStruct((t, D_MODEL), F32),
        compiler_params=_cparams(1),
        name="final_norm",
    )(x, g)


def kernel(x_prompt, x_sample, cache_k, cache_v, state_ssm_re, state_ssm_im, norm1_g, w_in, ssm_a_re, ssm_a_im, ssm_log_dt, ssm_b_re, ssm_b_im, ssm_c_re, ssm_c_im, ssm_d, ssm_glu_w, attn_rel_bias, attn_w_o, w_out, norm2_g, peer_w_q, peer_sub_keys, peer_u, peer_v, final_g):
    bp, lp, _ = x_prompt.shape
    bs, ls, _ = x_sample.shape
    depth = w_in.shape[0]
    tp = bp * lp
    kv_win = cache_k.shape[2]
    keep = min(LEFT_CHUNKS * CHUNK, lp)
    x = jnp.concatenate([x_prompt.reshape(tp, D_MODEL), x_sample.reshape(bs * ls, D_MODEL)], axis=0)
    t = x.shape[0]
    assert t % TOKEN_TILE == 0 and lp % SCAN_CHUNK == 0 and lp % CHUNK == 0 and kv_win == LEFT_CHUNKS * CHUNK

    zeros_state = jnp.zeros((bp, N_SSM), F32)
    outs = {n: [] for n in ("p_re", "p_im", "p_k", "p_v", "s_re", "s_im", "s_k", "s_v")}
    for l in range(depth):
        u, q, kv, kvb, gates = _inproj(x, norm1_g[l][None], w_in[l].astype(BF16))

        ssm_args = (ssm_a_re[l], ssm_a_im[l], ssm_log_dt[l], ssm_b_re[l], ssm_b_im[l], ssm_c_re[l], ssm_c_im[l])
        dskip = ssm_d[l][None]
        bcat, apr, api, ccat = _ssm_params(*ssm_args, SCAN_CHUNK)
        y_p, hr_p, hi_p = _ssm(u[:tp].reshape(bp, lp, D_SSM), zeros_state, zeros_state,
                               bcat, apr, api, ccat, dskip, SCAN_CHUNK)
        y_s, hr_s, hi_s = _ssm(u[tp:].reshape(bs, ls, D_SSM), state_ssm_re[l].reshape(bs, N_SSM),
                               state_ssm_im[l].reshape(bs, N_SSM), bcat, apr[:ls], api[:ls], ccat, dskip, ls)
        y = jnp.concatenate([y_p.reshape(tp, D_SSM), y_s.reshape(bs * ls, D_SSM)], axis=0)

        pad = ((0, 0), (LEFT_CHUNKS * CHUNK, 0), (0, 0))
        kb_p = jnp.pad(kvb[:tp, :D_ATTN].reshape(bp, lp, D_ATTN), pad)
        vb_p = jnp.pad(kvb[:tp, D_ATTN:].reshape(bp, lp, D_ATTN), pad)
        w_p = (LEFT_CHUNKS + 1) * CHUNK
        att_p = _attention(q[:tp].reshape(bp, lp, D_ATTN), kb_p, vb_p,
                           _bias_table(attn_rel_bias[l], CHUNK, w_p), CHUNK, w_p, True)
        kb_s = jnp.concatenate([cache_k[l].reshape(bs, kv_win, D_ATTN).astype(BF16),
                                kvb[tp:, :D_ATTN].reshape(bs, ls, D_ATTN)], axis=1)
        vb_s = jnp.concatenate([cache_v[l].reshape(bs, kv_win, D_ATTN).astype(BF16),
                                kvb[tp:, D_ATTN:].reshape(bs, ls, D_ATTN)], axis=1)
        w_s = kv_win + ls
        att_s = _attention(q[tp:].reshape(bs, ls, D_ATTN), kb_s, vb_s,
                           _bias_table(attn_rel_bias[l], ls, w_s), ls, w_s, False)
        att = jnp.concatenate([att_p.reshape(tp, D_ATTN), att_s.reshape(bs * ls, D_ATTN)], axis=0)

        x1, xn, qp = _mix(x, y, att, gates, ssm_glu_w[l].astype(BF16), attn_w_o[l].astype(BF16),
                          w_out[l].astype(BF16), norm2_g[l][None], peer_w_q[l].astype(BF16))
        keys = peer_sub_keys[l].reshape(2 * PEER_HEADS, N_KEYS, PEER_HALF).astype(BF16)
        rows_t, g_t = _route(qp, keys)
        rows = rows_t.T
        wgt_t = _peer_u(rows, _pack_table(peer_u[l]), xn.T, g_t)
        x = _peer_v(rows, _pack_table(peer_v[l]), wgt_t.T, x1)

        kv_p = kv[:tp].reshape(bp, lp, 2, N_HEADS, HEAD_DIM)[:, lp - keep:]
        kv_s = kv[tp:].reshape(bs, ls, 2, N_HEADS, HEAD_DIM)
        outs["p_re"].append(hr_p.reshape(bp, N_GROUPS, SSM_STATE))
        outs["p_im"].append(hi_p.reshape(bp, N_GROUPS, SSM_STATE))
        outs["p_k"].append(kv_p[:, :, 0])
        outs["p_v"].append(kv_p[:, :, 1])
        outs["s_re"].append(hr_s.reshape(bs, N_GROUPS, SSM_STATE))
        outs["s_im"].append(hi_s.reshape(bs, N_GROUPS, SSM_STATE))
        outs["s_k"].append(kv_s[:, :, 0])
        outs["s_v"].append(kv_s[:, :, 1])

    y = _final_norm(x, final_g[None])
    st = {n: jnp.stack(v) for n, v in outs.items()}
    return (y[:tp].reshape(bp, lp, D_MODEL), y[tp:].reshape(bs, ls, D_MODEL),
            st["p_re"], st["p_im"], st["p_k"], st["p_v"], st["s_re"], st["s_im"], st["s_k"], st["s_v"])
```

```python
import functools
import math

import jax
import jax.numpy as jnp
import numpy as np
from jax import lax
from jax.experimental import pallas as pl
from jax.experimental.pallas import tpu as pltpu

F32 = jnp.float32
BF16 = jnp.bfloat16

D_MODEL = 1024
CHUNK = 64
D_SSM = 512
SSM_GROUP = 16
N_GROUPS = D_SSM // SSM_GROUP
SSM_STATE = 64
N_SSM = N_GROUPS * SSM_STATE
N_HEADS = 8
HEAD_DIM = 64
D_ATTN = N_HEADS * HEAD_DIM
LEFT_CHUNKS = 8
REL_CLIP = 128
ATTN_SCALE = HEAD_DIM ** -0.5
N_KEYS = 128
N_EXPERTS = N_KEYS * N_KEYS
PEER_HEADS = 8
PEER_DK = 256
PEER_HALF = PEER_DK // 2
PEER_TOPK = 16
PEER_SLOTS = PEER_HEADS * PEER_TOPK
RMS_EPS = 1e-6
D_IN = D_SSM + 3 * D_ATTN + 2 * D_MODEL

VMEM_LIMIT_BYTES = 56 * 1024 * 1024
TOKEN_TILE = 256
ROUTE_TILE = 128
PEER_TOKENS = 128
PEER_UNROLL = 8
SCAN_CHUNK = 256
NEG_INF = float("-inf")

ROW_WORDS = 4
G_STRIDE = 136
G_ROWS = ROW_WORDS * G_STRIDE
IDX_QUAD = 4

PEER_CAND = [(i, j) for i in range(PEER_TOPK) for j in range(PEER_TOPK) if (i + 1) * (j + 1) <= PEER_TOPK]
N_CAND_ROWS = 56


def _cparams(n_axes):
    return pltpu.CompilerParams(dimension_semantics=("arbitrary",) * n_axes,
                                vmem_limit_bytes=VMEM_LIMIT_BYTES)


def _const_spec(shape):
    return pl.BlockSpec(shape, lambda *_: (0,) * len(shape))


def _inproj_kernel(x_ref, g_ref, w_ref, u_ref, q_ref, kv_ref, kvb_ref, gate_ref):
    x = x_ref[...]
    ms = jnp.mean(x * x, axis=-1, keepdims=True)
    xn = (x * lax.rsqrt(ms + RMS_EPS) * g_ref[...]).astype(BF16)

    def proj(lo, hi):
        return jnp.dot(xn, w_ref[:, lo:hi], preferred_element_type=F32)

    u_ref[...] = proj(0, D_SSM).astype(BF16)
    q_ref[...] = proj(D_SSM, D_SSM + D_ATTN).astype(BF16)
    for c in range(2):
        lo = D_SSM + D_ATTN + c * D_ATTN
        kv = proj(lo, lo + D_ATTN)
        kv_ref[:, c * D_ATTN:(c + 1) * D_ATTN] = kv
        kvb_ref[:, c * D_ATTN:(c + 1) * D_ATTN] = kv.astype(BF16)
    for c in range(4):
        lo = D_SSM + 3 * D_ATTN + c * 512
        gate_ref[:, c * 512:(c + 1) * 512] = jax.nn.sigmoid(proj(lo, lo + 512)).astype(BF16)


def _inproj(x, g, w_bf):
    t = x.shape[0]
    tm = TOKEN_TILE
    row = lambda n: pl.BlockSpec((tm, n), lambda i: (i, 0))
    return pl.pallas_call(
        _inproj_kernel,
        grid=(t // tm,),
        in_specs=[row(D_MODEL), _const_spec((1, D_MODEL)), _const_spec((D_MODEL, D_IN))],
        out_specs=[row(D_SSM), row(D_ATTN), row(2 * D_ATTN), row(2 * D_ATTN), row(2 * D_MODEL)],
        out_shape=[jax.ShapeDtypeStruct((t, D_SSM), BF16), jax.ShapeDtypeStruct((t, D_ATTN), BF16),
                   jax.ShapeDtypeStruct((t, 2 * D_ATTN), F32), jax.ShapeDtypeStruct((t, 2 * D_ATTN), BF16),
                   jax.ShapeDtypeStruct((t, 2 * D_MODEL), BF16)],
        compiler_params=_cparams(1),
        name="inproj",
    )(x, g, w_bf)


def _ssm_kernel(lc, u_ref, bcat_ref, apr_ref, api_ref, ccat_ref, d_ref, h0r_ref, h0i_ref,
                y_ref, hr_ref, hi_ref, re_ref, im_ref, cr_ref, ci_ref):
    c = pl.program_id(1)

    @pl.when(c == 0)
    def _():
        cr_ref[...] = h0r_ref[0]
        ci_ref[...] = h0i_ref[0]

    u = u_ref[0]
    bu = jnp.dot(u, bcat_ref[...], preferred_element_type=F32)
    re_ref[...] = bu[:, :N_SSM]
    im_ref[...] = bu[:, N_SSM:]

    row = lax.broadcasted_iota(jnp.int32, (lc, 128), 0)

    def lane_tile(j, _):
        sl = pl.ds(pl.multiple_of(j * 128, 128), 128)
        hr = re_ref[:, sl]
        hi = im_ref[:, sl]
        s = 1
        while s < lc:
            ar = apr_ref[s - 1:s, sl]
            ai = api_ref[s - 1:s, sl]
            keep = row >= s
            sr = jnp.where(keep, pltpu.roll(hr, s, 0), 0.0)
            si = jnp.where(keep, pltpu.roll(hi, s, 0), 0.0)
            hr, hi = hr + (ar * sr - ai * si), hi + (ar * si + ai * sr)
            s *= 2
        pr = apr_ref[:, sl]
        pi = api_ref[:, sl]
        c_r = cr_ref[:, sl]
        c_i = ci_ref[:, sl]
        hr = hr + (pr * c_r - pi * c_i)
        hi = hi + (pr * c_i + pi * c_r)
        re_ref[:, sl] = hr
        im_ref[:, sl] = hi
        cr_ref[:, sl] = hr[lc - 1:lc]
        ci_ref[:, sl] = hi[lc - 1:lc]
        return 0

    lax.fori_loop(0, N_SSM // 128, lane_tile, 0)

    hr_ref[0] = cr_ref[...]
    hi_ref[0] = ci_ref[...]
    y = jnp.dot(re_ref[...].astype(BF16), ccat_ref[:N_SSM], preferred_element_type=F32)
    y = y + jnp.dot(im_ref[...].astype(BF16), ccat_ref[N_SSM:], preferred_element_type=F32)
    y_ref[0] = (y + d_ref[...] * u.astype(F32)).astype(BF16)


def _ssm(u, h0r, h0i, bcat, apr, api, ccat, dskip, lc):
    b, l, _ = u.shape
    st = jax.ShapeDtypeStruct((b, 1, N_SSM), F32)
    state_spec = pl.BlockSpec((1, 1, N_SSM), lambda i, c: (i, 0, 0))
    return pl.pallas_call(
        functools.partial(_ssm_kernel, lc),
        grid=(b, l // lc),
        in_specs=[pl.BlockSpec((1, lc, D_SSM), lambda i, c: (i, c, 0)),
                  _const_spec((D_SSM, 2 * N_SSM)), _const_spec((lc, N_SSM)), _const_spec((lc, N_SSM)),
                  _const_spec((2 * N_SSM, D_SSM)), _const_spec((1, D_SSM)), state_spec, state_spec],
        out_specs=[pl.BlockSpec((1, lc, D_SSM), lambda i, c: (i, c, 0)), state_spec, state_spec],
        out_shape=[jax.ShapeDtypeStruct((b, l, D_SSM), BF16), st, st],
        scratch_shapes=[pltpu.VMEM((lc, N_SSM), F32), pltpu.VMEM((lc, N_SSM), F32),
                        pltpu.VMEM((1, N_SSM), F32), pltpu.VMEM((1, N_SSM), F32)],
        compiler_params=_cparams(2),
        name=f"ssm_scan_{lc}",
    )(u, bcat, apr, api, ccat, dskip, h0r.reshape(b, 1, N_SSM), h0i.reshape(b, 1, N_SSM))


def _ssm_params(a_re, a_im, log_dt, b_re, b_im, c_re, c_im, lc):
    a = lax.complex(a_re, a_im)
    adt = a * jnp.exp(log_dt)[:, None]
    a_bar = jnp.exp(adt)
    b_bar = ((a_bar - 1.0) / a)[..., None] * lax.complex(b_re, b_im)
    eye = jnp.eye(N_GROUPS, dtype=F32)
    bre = jnp.einsum("gpc,gh->gchp", jnp.real(b_bar), eye).reshape(D_SSM, N_SSM)
    bim = jnp.einsum("gpc,gh->gchp", jnp.imag(b_bar), eye).reshape(D_SSM, N_SSM)
    bcat = jnp.concatenate([bre, bim], axis=1).astype(BF16)
    cre = jnp.einsum("gcp,gh->gphc", c_re, eye).reshape(N_SSM, D_SSM)
    cim = jnp.einsum("gcp,gh->gphc", c_im, eye).reshape(N_SSM, D_SSM)
    ccat = jnp.concatenate([cre, -cim], axis=0).astype(BF16)
    steps = jnp.arange(1, lc + 1, dtype=F32)[:, None, None]
    apow = jnp.exp(adt[None] * steps).reshape(lc, N_SSM)
    return bcat, jnp.real(apow), jnp.imag(apow), ccat


def _attn_kernel(qc, w, masked, q_ref, k_ref, v_ref, bias_ref, o_ref):
    n = pl.program_id(1)
    q = q_ref[0]
    head = lax.broadcasted_iota(jnp.int32, (qc, D_ATTN), 1) // HEAD_DIM
    qs = jnp.concatenate([jnp.where(head == h, q, jnp.zeros_like(q)) for h in range(N_HEADS)], axis=0)
    start = pl.multiple_of(n * qc, qc)
    k = k_ref[0, pl.ds(start, w), :]
    v = v_ref[0, pl.ds(start, w), :]
    s = lax.dot_general(qs, k, (((1,), (1,)), ((), ())), preferred_element_type=F32)
    s = s * ATTN_SCALE + bias_ref[...]
    if masked:
        col = lax.broadcasted_iota(jnp.int32, (N_HEADS * qc, w), 1)
        s = jnp.where(col + n * qc >= LEFT_CHUNKS * CHUNK, s, jnp.finfo(F32).min)
    m = jnp.max(s, axis=-1, keepdims=True)
    p = jnp.exp(s - m)
    p = p / jnp.sum(p, axis=-1, keepdims=True)
    r = jnp.dot(p.astype(BF16), v, preferred_element_type=F32)
    o = jnp.zeros((qc, D_ATTN), F32)
    for h in range(N_HEADS):
        o = o + jnp.where(head == h, r[h * qc:(h + 1) * qc], 0.0)
    o_ref[0] = o.astype(BF16)


def _attention(q, k, v, bias, qc, w, masked):
    b, l, _ = q.shape
    lk = k.shape[1]
    kv_spec = pl.BlockSpec((1, lk, D_ATTN), lambda i, n: (i, 0, 0))
    return pl.pallas_call(
        functools.partial(_attn_kernel, qc, w, masked),
        grid=(b, l // qc),
        in_specs=[pl.BlockSpec((1, qc, D_ATTN), lambda i, n: (i, n, 0)), kv_spec, kv_spec,
                  _const_spec((N_HEADS * qc, w))],
        out_specs=pl.BlockSpec((1, qc, D_ATTN), lambda i, n: (i, n, 0)),
        out_shape=jax.ShapeDtypeStruct((b, l, D_ATTN), BF16),
        compiler_params=_cparams(2),
        name=f"band_attn_{qc}",
    )(q, k, v, bias)


def _bias_table(rel_bias, qc, w):
    rel = (jnp.arange(w) - LEFT_CHUNKS * CHUNK)[None, :] - jnp.arange(qc)[:, None]
    idx = jnp.clip(rel, -REL_CLIP, REL_CLIP) + REL_CLIP
    return rel_bias[:, idx].astype(F32).reshape(N_HEADS * qc, w)


def _mix_kernel(x_ref, y_ref, att_ref, gate_ref, glu_ref, wo_ref, wout_ref, g2_ref, wq_ref,
                x1_ref, xnt_ref, qp_ref):
    glu = jnp.dot(y_ref[...], glu_ref[...], preferred_element_type=F32)
    a = glu[:, :D_MODEL] * jax.nn.sigmoid(glu[:, D_MODEL:])
    b = jnp.dot(att_ref[...], wo_ref[...], preferred_element_type=F32)
    mixed = gate_ref[:, :D_MODEL].astype(F32) * a + gate_ref[:, D_MODEL:].astype(F32) * b
    x1 = x_ref[...] + jnp.dot(mixed.astype(BF16), wout_ref[...], preferred_element_type=F32)
    x1_ref[...] = x1
    ms = jnp.mean(x1 * x1, axis=-1, keepdims=True)
    xn = x1 * lax.rsqrt(ms + RMS_EPS) * g2_ref[...]
    xnt_ref[...] = xn.T.astype(BF16)
    qp_ref[...] = jnp.dot(xn.astype(BF16), wq_ref[...], preferred_element_type=F32).astype(BF16)


def _mix(x, y, att, gates, glu_w, w_o, w_out, g2, w_q):
    t = x.shape[0]
    tm = TOKEN_TILE
    row = lambda n: pl.BlockSpec((tm, n), lambda i: (i, 0))
    dq = PEER_HEADS * PEER_DK
    return pl.pallas_call(
        _mix_kernel,
        grid=(t // tm,),
        in_specs=[row(D_MODEL), row(D_SSM), row(D_ATTN), row(2 * D_MODEL),
                  _const_spec((D_SSM, 2 * D_MODEL)), _const_spec((D_ATTN, D_MODEL)),
                  _const_spec((D_MODEL, D_MODEL)), _const_spec((1, D_MODEL)), _const_spec((D_MODEL, dq))],
        out_specs=[row(D_MODEL), pl.BlockSpec((D_MODEL, tm), lambda i: (0, i)), row(dq)],
        out_shape=[jax.ShapeDtypeStruct((t, D_MODEL), F32), jax.ShapeDtypeStruct((D_MODEL, t), BF16),
                   jax.ShapeDtypeStruct((t, dq), BF16)],
        compiler_params=_cparams(1),
        name="mix",
    )(x, y, att, gates, glu_w, w_o, w_out, g2, w_q)


def _extract_max(s, row):
    m = jnp.max(s, axis=0, keepdims=True)
    pos = jnp.min(jnp.where(s == m, row, float(s.shape[0])), axis=0, keepdims=True)
    return m, pos, jnp.where(row == pos, NEG_INF, s)


def _route_kernel(qp_ref, keys_ref, rows_ref, g_ref, e_ref):
    tm = ROUTE_TILE
    row_k = lax.broadcasted_iota(jnp.int32, (N_KEYS, tm), 0).astype(F32)
    row_c = lax.broadcasted_iota(jnp.int32, (N_CAND_ROWS, tm), 0).astype(F32)
    row_o = lax.broadcasted_iota(jnp.int32, (PEER_TOPK, tm), 0)
    for h in range(PEER_HEADS):
        vals, ids = [], []
        for z in range(2):
            hz = 2 * h + z
            q = qp_ref[:, hz * PEER_HALF:(hz + 1) * PEER_HALF]
            s = lax.dot_general(keys_ref[hz], q, (((1,), (1,)), ((), ())),
                                preferred_element_type=F32)
            v_z, i_z = [], []
            for _ in range(PEER_TOPK):
                m, pos, s = _extract_max(s, row_k)
                v_z.append(m)
                i_z.append(pos)
            vals.append(v_z)
            ids.append(i_z)
        cand = jnp.full((N_CAND_ROWS, tm), NEG_INF, F32)
        cand_e = jnp.zeros((N_CAND_ROWS, tm), F32)
        for c, (i, j) in enumerate(PEER_CAND):
            cand = jnp.where(row_c == float(c), vals[0][i] + vals[1][j], cand)
            cand_e = jnp.where(row_c == float(c), ids[0][i] * float(N_KEYS) + ids[1][j], cand_e)
        best = jnp.zeros((PEER_TOPK, tm), F32)
        best_e = jnp.zeros((PEER_TOPK, tm), F32)
        top = None
        for k in range(PEER_TOPK):
            m, pos, cand = _extract_max(cand, row_c)
            e = jnp.max(jnp.where(row_c == pos, cand_e, -1.0), axis=0, keepdims=True)
            top = m if top is None else top
            best = jnp.where(row_o == k, jnp.exp(m - top), best)
            best_e = jnp.where(row_o == k, e, best_e)
        e_ref[h * PEER_TOPK:(h + 1) * PEER_TOPK, :] = best_e
        g_ref[h * PEER_TOPK:(h + 1) * PEER_TOPK, :] = best / jnp.sum(best, axis=0, keepdims=True)
    rows_ref[...] = e_ref[...].T.astype(jnp.int32) * ROW_WORDS


def _route(qp, keys_bf):
    t = qp.shape[0]
    tm = ROUTE_TILE
    out = pl.BlockSpec((PEER_SLOTS, tm), lambda i: (0, i))
    return pl.pallas_call(
        _route_kernel,
        grid=(t // tm,),
        in_specs=[pl.BlockSpec((tm, PEER_HEADS * PEER_DK), lambda i: (i, 0)),
                  _const_spec((2 * PEER_HEADS, N_KEYS, PEER_HALF))],
        out_specs=[pl.BlockSpec((tm, PEER_SLOTS), lambda i: (i, 0)), out],
        out_shape=[jax.ShapeDtypeStruct((t, PEER_SLOTS), jnp.int32),
                   jax.ShapeDtypeStruct((PEER_SLOTS, t), F32)],
        scratch_shapes=[pltpu.VMEM((PEER_SLOTS, tm), F32)],
        compiler_params=_cparams(1),
        name="peer_route",
    )(qp, keys_bf)


def _pack_table(tab):
    tb = tab.astype(BF16)
    hi = lax.bitcast_convert_type(tb[:, :512], jnp.uint16).astype(jnp.uint32)
    lo = lax.bitcast_convert_type(tb[:, 512:], jnp.uint16).astype(jnp.uint32)
    return ((hi << 16) | lo).reshape(N_EXPERTS * ROW_WORDS, 128)


def _gather_group(idx_ref, g, tab_ref, bufs):
    for u, buf in enumerate(bufs):
        for a in range(PEER_SLOTS // IDX_QUAD):
            quad = idx_ref.at[g * PEER_UNROLL + u, pl.ds(a * IDX_QUAD, IDX_QUAD)]
            for b in range(IDX_QUAD):
                k = a * IDX_QUAD + b
                row = pl.multiple_of(quad[b], ROW_WORDS)
                buf[pl.ds(k, ROW_WORDS, stride=G_STRIDE), :] = tab_ref[pl.ds(row, ROW_WORDS), :]


def _lane_tile(buf_ref, j):
    word = buf_ref[j * G_STRIDE:j * G_STRIDE + PEER_SLOTS, :]
    hi = pltpu.bitcast(word & jnp.uint32(0xFFFF0000), F32)
    lo = pltpu.bitcast(word << 16, F32)
    return hi, lo


def _token_pipeline(idx_ref, tab_ref, bufs, compute_group, split_regions):
    n_groups = PEER_TOKENS // PEER_UNROLL
    halves = (bufs[:PEER_UNROLL], bufs[PEER_UNROLL:])
    _gather_group(idx_ref, 0, tab_ref, halves[0])

    def body(i, _):
        for half in range(2):
            g = 2 * i + half

            def pair(g=g, half=half):
                _gather_group(idx_ref, jnp.minimum(g + 1, n_groups - 1), tab_ref, halves[1 - half])
                compute_group(g * PEER_UNROLL, halves[half])

            if split_regions:
                pl.when(g < n_groups)(pair)
            else:
                pair()
        return 0

    lax.fori_loop(0, n_groups // 2, body, 0)


def _gathered_rows(buf):
    tiles = [_lane_tile(buf, j) for j in range(ROW_WORDS)]
    return jnp.concatenate([tl[0] for tl in tiles] + [tl[1] for tl in tiles], axis=1)


def _peer_u_kernel(idx_ref, tab_ref, xt_ref, g_ref, w_ref, xw_ref, act_ref, *bufs):
    lane = lax.broadcasted_iota(jnp.int32, (PEER_SLOTS, PEER_TOKENS), 1)
    xw_ref[...] = xt_ref[...].astype(F32)

    def compute_group(t0, bufs):
        lhs = jnp.concatenate([_gathered_rows(b) for b in bufs], axis=0)
        r = jnp.dot(lhs, xw_ref[...], preferred_element_type=F32)
        for u in range(PEER_UNROLL):
            pltpu.store(act_ref, r[u * PEER_SLOTS:(u + 1) * PEER_SLOTS], mask=lane == t0 + u)

    _token_pipeline(idx_ref, tab_ref, bufs, compute_group, split_regions=True)
    act = act_ref[...]
    gelu = 0.5 * act * (1.0 + lax.erf(act * (1.0 / math.sqrt(2.0))))
    w_ref[...] = g_ref[...] * gelu


def _peer_v_kernel(idx_ref, tab_ref, wgt_ref, x_ref, o_ref, wt_ref, *bufs):
    sub = lax.broadcasted_iota(jnp.int32, (8, PEER_SLOTS), 0)
    wt_ref[...] = wgt_ref[...].T

    def compute_group(t0, bufs):
        for u, buf in enumerate(bufs):
            w = wt_ref[pl.ds(t0 + u, 1), :]
            w0 = w.astype(BF16).astype(F32)
            r1 = w - w0
            w1 = r1.astype(BF16).astype(F32)
            w2 = r1 - w1
            lhs = jnp.where(sub == 0, w0, jnp.where(sub == 1, w1, jnp.where(sub == 2, w2, 0.0)))
            r = jnp.dot(lhs, _gathered_rows(buf), preferred_element_type=F32)
            o_ref[pl.ds(t0 + u, 1), :] = x_ref[pl.ds(t0 + u, 1), :] + (r[0:1] + r[1:2] + r[2:3])

    _token_pipeline(idx_ref, tab_ref, bufs, compute_group, split_regions=False)


def _idx_spec():
    return pl.BlockSpec((PEER_TOKENS, PEER_SLOTS), lambda i: (i, 0), memory_space=pltpu.SMEM)


def _table_spec():
    return pl.BlockSpec((N_EXPERTS * ROW_WORDS, 128), lambda i: (0, 0), pipeline_mode=pl.Buffered(1))


def _gather_buffers():
    return [pltpu.VMEM((G_ROWS, 128), jnp.uint32) for _ in range(2 * PEER_UNROLL)]


def _peer_u(rows_t, tab, xt, g_t):
    t = xt.shape[1]
    col = pl.BlockSpec((PEER_SLOTS, PEER_TOKENS), lambda i: (0, i))
    return pl.pallas_call(
        _peer_u_kernel,
        grid=(t // PEER_TOKENS,),
        in_specs=[_idx_spec(), _table_spec(), pl.BlockSpec((D_MODEL, PEER_TOKENS), lambda i: (0, i)), col],
        out_specs=col,
        out_shape=jax.ShapeDtypeStruct((PEER_SLOTS, t), F32),
        scratch_shapes=[pltpu.VMEM((D_MODEL, PEER_TOKENS), F32),
                        pltpu.VMEM((PEER_SLOTS, PEER_TOKENS), F32)] + _gather_buffers(),
        compiler_params=_cparams(1),
        name="peer_u",
    )(rows_t, tab, xt, g_t)


def _peer_v(rows_t, tab, wgt_t, x):
    t = x.shape[0]
    row = pl.BlockSpec((PEER_TOKENS, D_MODEL), lambda i: (i, 0))
    col = pl.BlockSpec((PEER_SLOTS, PEER_TOKENS), lambda i: (0, i))
    return pl.pallas_call(
        _peer_v_kernel,
        grid=(t // PEER_TOKENS,),
        in_specs=[_idx_spec(), _table_spec(), col, row],
        out_specs=row,
        out_shape=jax.ShapeDtypeStruct((t, D_MODEL), F32),
        scratch_shapes=[pltpu.VMEM((PEER_TOKENS, PEER_SLOTS), F32)] + _gather_buffers(),
        compiler_params=_cparams(1),
        name="peer_v",
    )(rows_t, tab, wgt_t, x)


def _final_norm_kernel(x_ref, g_ref, y_ref):
    x = x_ref[...]
    ms = jnp.mean(x * x, axis=-1, keepdims=True)
    y_ref[...] = x * lax.rsqrt(ms + RMS_EPS) * g_ref[...]


def _final_norm(x, g):
    t = x.shape[0]
    tm = TOKEN_TILE
    return pl.pallas_call(
        _final_norm_kernel,
        grid=(t // tm,),
        in_specs=[pl.BlockSpec((tm, D_MODEL), lambda i: (i, 0)), _const_spec((1, D_MODEL))],
        out_specs=pl.BlockSpec((tm, D_MODEL), lambda i: (i, 0)),
        out_shape=jax.ShapeDtypeStruct((t, D_MODEL), F32),
        compiler_params=_cparams(1),
        name="final_norm",
    )(x, g)


def kernel(x_prompt, x_sample, cache_k, cache_v, state_ssm_re, state_ssm_im, norm1_g, w_in, ssm_a_re, ssm_a_im, ssm_log_dt, ssm_b_re, ssm_b_im, ssm_c_re, ssm_c_im, ssm_d, ssm_glu_w, attn_rel_bias, attn_w_o, w_out, norm2_g, peer_w_q, peer_sub_keys, peer_u, peer_v, final_g):
    bp, lp, _ = x_prompt.shape
    bs, ls, _ = x_sample.shape
    depth = w_in.shape[0]
    tp = bp * lp
    kv_win = cache_k.shape[2]
    keep = min(LEFT_CHUNKS * CHUNK, lp)
    x = jnp.concatenate([x_prompt.reshape(tp, D_MODEL), x_sample.reshape(bs * ls, D_MODEL)], axis=0)
    t = x.shape[0]
    assert t % TOKEN_TILE == 0 and lp % SCAN_CHUNK == 0 and lp % CHUNK == 0 and kv_win == LEFT_CHUNKS * CHUNK

    zeros_state = jnp.zeros((bp, N_SSM), F32)
    outs = {n: [] for n in ("p_re", "p_im", "p_k", "p_v", "s_re", "s_im", "s_k", "s_v")}
    for l in range(depth):
        u, q, kv, kvb, gates = _inproj(x, norm1_g[l][None], w_in[l].astype(BF16))

        ssm_args = (ssm_a_re[l], ssm_a_im[l], ssm_log_dt[l], ssm_b_re[l], ssm_b_im[l], ssm_c_re[l], ssm_c_im[l])
        dskip = ssm_d[l][None]
        bcat, apr, api, ccat = _ssm_params(*ssm_args, SCAN_CHUNK)
        y_p, hr_p, hi_p = _ssm(u[:tp].reshape(bp, lp, D_SSM), zeros_state, zeros_state,
                               bcat, apr, api, ccat, dskip, SCAN_CHUNK)
        y_s, hr_s, hi_s = _ssm(u[tp:].reshape(bs, ls, D_SSM), state_ssm_re[l].reshape(bs, N_SSM),
                               state_ssm_im[l].reshape(bs, N_SSM), bcat, apr[:ls], api[:ls], ccat, dskip, ls)
        y = jnp.concatenate([y_p.reshape(tp, D_SSM), y_s.reshape(bs * ls, D_SSM)], axis=0)

        pad = ((0, 0), (LEFT_CHUNKS * CHUNK, 0), (0, 0))
        kb_p = jnp.pad(kvb[:tp, :D_ATTN].reshape(bp, lp, D_ATTN), pad)
        vb_p = jnp.pad(kvb[:tp, D_ATTN:].reshape(bp, lp, D_ATTN), pad)
        w_p = (LEFT_CHUNKS + 1) * CHUNK
        att_p = _attention(q[:tp].reshape(bp, lp, D_ATTN), kb_p, vb_p,
                           _bias_table(attn_rel_bias[l], CHUNK, w_p), CHUNK, w_p, True)
        kb_s = jnp.concatenate([cache_k[l].reshape(bs, kv_win, D_ATTN).astype(BF16),
                                kvb[tp:, :D_ATTN].reshape(bs, ls, D_ATTN)], axis=1)
        vb_s = jnp.concatenate([cache_v[l].reshape(bs, kv_win, D_ATTN).astype(BF16),
                                kvb[tp:, D_ATTN:].reshape(bs, ls, D_ATTN)], axis=1)
        w_s = kv_win + ls
        att_s = _attention(q[tp:].reshape(bs, ls, D_ATTN), kb_s, vb_s,
                           _bias_table(attn_rel_bias[l], ls, w_s), ls, w_s, False)
        att = jnp.concatenate([att_p.reshape(tp, D_ATTN), att_s.reshape(bs * ls, D_ATTN)], axis=0)

        x1, xn_t, qp = _mix(x, y, att, gates, ssm_glu_w[l].astype(BF16), attn_w_o[l].astype(BF16),
                          w_out[l].astype(BF16), norm2_g[l][None], peer_w_q[l].astype(BF16))
        keys = peer_sub_keys[l].reshape(2 * PEER_HEADS, N_KEYS, PEER_HALF).astype(BF16)
        rows, g_t = _route(qp, keys)
        wgt_t = _peer_u(rows, _pack_table(peer_u[l]), xn_t, g_t)
        x = _peer_v(rows, _pack_table(peer_v[l]), wgt_t, x1)

        kv_p = kv[:tp].reshape(bp, lp, 2, N_HEADS, HEAD_DIM)[:, lp - keep:]
        kv_s = kv[tp:].reshape(bs, ls, 2, N_HEADS, HEAD_DIM)
        outs["p_re"].append(hr_p.reshape(bp, N_GROUPS, SSM_STATE))
        outs["p_im"].append(hi_p.reshape(bp, N_GROUPS, SSM_STATE))
        outs["p_k"].append(kv_p[:, :, 0])
        outs["p_v"].append(kv_p[:, :, 1])
        outs["s_re"].append(hr_s.reshape(bs, N_GROUPS, SSM_STATE))
        outs["s_im"].append(hi_s.reshape(bs, N_GROUPS, SSM_STATE))
        outs["s_k"].append(kv_s[:, :, 0])
        outs["s_v"].append(kv_s[:, :, 1])

    y = _final_norm(x, final_g[None])
    st = {n: jnp.stack(v) for n, v in outs.items()}
    return (y[:tp].reshape(bp, lp, D_MODEL), y[tp:].reshape(bs, ls, D_MODEL),
            st["p_re"], st["p_im"], st["p_k"], st["p_v"], st["s_re"], st["s_im"], st["s_k"], st["s_v"])
```

```python
import functools
import math

import jax
import jax.numpy as jnp
import numpy as np
from jax import lax
from jax.experimental import pallas as pl
from jax.experimental.pallas import tpu as pltpu

F32 = jnp.float32
BF16 = jnp.bfloat16

D_MODEL = 1024
CHUNK = 64
D_SSM = 512
SSM_GROUP = 16
N_GROUPS = D_SSM // SSM_GROUP
SSM_STATE = 64
N_SSM = N_GROUPS * SSM_STATE
N_HEADS = 8
HEAD_DIM = 64
D_ATTN = N_HEADS * HEAD_DIM
LEFT_CHUNKS = 8
REL_CLIP = 128
ATTN_SCALE = HEAD_DIM ** -0.5
N_KEYS = 128
N_EXPERTS = N_KEYS * N_KEYS
PEER_HEADS = 8
PEER_DK = 256
PEER_HALF = PEER_DK // 2
PEER_TOPK = 16
PEER_SLOTS = PEER_HEADS * PEER_TOPK
RMS_EPS = 1e-6
D_IN = D_SSM + 3 * D_ATTN + 2 * D_MODEL

VMEM_LIMIT_BYTES = 56 * 1024 * 1024
TOKEN_TILE = 256
ROUTE_TILE = 128
PEER_TOKENS = 128
PEER_UNROLL = 8
SCAN_CHUNK = 256
NEG_INF = float("-inf")

ROW_WORDS = 4
G_STRIDE = 136
G_ROWS = ROW_WORDS * G_STRIDE
IDX_QUAD = 4

PEER_CAND = [(i, j) for i in range(PEER_TOPK) for j in range(PEER_TOPK) if (i + 1) * (j + 1) <= PEER_TOPK]
N_CAND_ROWS = 56


def _cparams(n_axes):
    return pltpu.CompilerParams(dimension_semantics=("arbitrary",) * n_axes,
                                vmem_limit_bytes=VMEM_LIMIT_BYTES)


def _const_spec(shape):
    return pl.BlockSpec(shape, lambda *_: (0,) * len(shape))


def _inproj_kernel(x_ref, g_ref, w_ref, u_ref, q_ref, kv_ref, kvb_ref, gate_ref):
    x = x_ref[...]
    ms = jnp.mean(x * x, axis=-1, keepdims=True)
    xn = (x * lax.rsqrt(ms + RMS_EPS) * g_ref[...]).astype(BF16)

    def proj(lo, hi):
        return jnp.dot(xn, w_ref[:, lo:hi], preferred_element_type=F32)

    u_ref[...] = proj(0, D_SSM).astype(BF16)
    q_ref[...] = proj(D_SSM, D_SSM + D_ATTN).astype(BF16)
    for c in range(2):
        lo = D_SSM + D_ATTN + c * D_ATTN
        kv = proj(lo, lo + D_ATTN)
        kv_ref[:, c * D_ATTN:(c + 1) * D_ATTN] = kv
        kvb_ref[:, c * D_ATTN:(c + 1) * D_ATTN] = kv.astype(BF16)
    for c in range(4):
        lo = D_SSM + 3 * D_ATTN + c * 512
        gate_ref[:, c * 512:(c + 1) * 512] = jax.nn.sigmoid(proj(lo, lo + 512)).astype(BF16)


def _inproj(x, g, w_bf):
    t = x.shape[0]
    tm = TOKEN_TILE
    row = lambda n: pl.BlockSpec((tm, n), lambda i: (i, 0))
    return pl.pallas_call(
        _inproj_kernel,
        grid=(t // tm,),
        in_specs=[row(D_MODEL), _const_spec((1, D_MODEL)), _const_spec((D_MODEL, D_IN))],
        out_specs=[row(D_SSM), row(D_ATTN), row(2 * D_ATTN), row(2 * D_ATTN), row(2 * D_MODEL)],
        out_shape=[jax.ShapeDtypeStruct((t, D_SSM), BF16), jax.ShapeDtypeStruct((t, D_ATTN), BF16),
                   jax.ShapeDtypeStruct((t, 2 * D_ATTN), F32), jax.ShapeDtypeStruct((t, 2 * D_ATTN), BF16),
                   jax.ShapeDtypeStruct((t, 2 * D_MODEL), BF16)],
        compiler_params=_cparams(1),
        name="inproj",
    )(x, g, w_bf)


def _ssm_kernel(lc, u_ref, bcat_ref, apr_ref, api_ref, ccat_ref, d_ref, h0r_ref, h0i_ref, *rest):
    y_ref, hr_ref, hi_ref, re_ref, im_ref, cr_ref, ci_ref = rest[-7:]
    c = pl.program_id(1)

    @pl.when(c == 0)
    def _():
        cr_ref[...] = h0r_ref[0]
        ci_ref[...] = h0i_ref[0]

    u = u_ref[...]
    bu = jnp.dot(u, bcat_ref[...], preferred_element_type=F32)
    re_ref[...] = bu[:, :N_SSM]
    im_ref[...] = bu[:, N_SSM:]

    row = lax.broadcasted_iota(jnp.int32, (lc, 128), 0)

    def lane_tile(j, _):
        sl = pl.ds(pl.multiple_of(j * 128, 128), 128)
        hr = re_ref[:, sl]
        hi = im_ref[:, sl]
        s = 1
        while s < lc:
            ar = apr_ref[s - 1:s, sl]
            ai = api_ref[s - 1:s, sl]
            keep = row >= s
            sr = jnp.where(keep, pltpu.roll(hr, s, 0), 0.0)
            si = jnp.where(keep, pltpu.roll(hi, s, 0), 0.0)
            hr, hi = hr + (ar * sr - ai * si), hi + (ar * si + ai * sr)
            s *= 2
        pr = apr_ref[:, sl]
        pi = api_ref[:, sl]
        c_r = cr_ref[:, sl]
        c_i = ci_ref[:, sl]
        hr = hr + (pr * c_r - pi * c_i)
        hi = hi + (pr * c_i + pi * c_r)
        re_ref[:, sl] = hr
        im_ref[:, sl] = hi
        cr_ref[:, sl] = hr[lc - 1:lc]
        ci_ref[:, sl] = hi[lc - 1:lc]
        return 0

    lax.fori_loop(0, N_SSM // 128, lane_tile, 0)

    hr_ref[0] = cr_ref[...]
    hi_ref[0] = ci_ref[...]
    y = jnp.dot(re_ref[...].astype(BF16), ccat_ref[:N_SSM], preferred_element_type=F32)
    y = y + jnp.dot(im_ref[...].astype(BF16), ccat_ref[N_SSM:], preferred_element_type=F32)
    y_ref[...] = (y + d_ref[...] * u.astype(F32)).astype(BF16)


def _seq_rows_spec(rows, width, first_row, seq_len):
    per_seq = seq_len // rows
    base = first_row // rows
    return pl.BlockSpec((rows, width), lambda i, n: (base + i * per_seq + n, 0))


def _alias_prev(prev, n_inputs):
    if prev is None:
        return [], [], {}
    return [prev], [pl.BlockSpec(memory_space=pl.ANY)], {n_inputs: 0}


def _ssm(u, prev_y, b, l, first_row, h0r, h0i, bcat, apr, api, ccat, dskip, lc):
    st = jax.ShapeDtypeStruct((b, 1, N_SSM), F32)
    state_spec = pl.BlockSpec((1, 1, N_SSM), lambda i, c: (i, 0, 0))
    rows = _seq_rows_spec(lc, D_SSM, first_row, l)
    extra, extra_specs, aliases = _alias_prev(prev_y, 8)
    return pl.pallas_call(
        functools.partial(_ssm_kernel, lc),
        grid=(b, l // lc),
        in_specs=[rows, _const_spec((D_SSM, 2 * N_SSM)), _const_spec((lc, N_SSM)), _const_spec((lc, N_SSM)),
                  _const_spec((2 * N_SSM, D_SSM)), _const_spec((1, D_SSM)), state_spec, state_spec] + extra_specs,
        out_specs=[rows, state_spec, state_spec],
        out_shape=[jax.ShapeDtypeStruct(u.shape, BF16), st, st],
        scratch_shapes=[pltpu.VMEM((lc, N_SSM), F32), pltpu.VMEM((lc, N_SSM), F32),
                        pltpu.VMEM((1, N_SSM), F32), pltpu.VMEM((1, N_SSM), F32)],
        input_output_aliases=aliases,
        compiler_params=_cparams(2),
        name=f"ssm_scan_{lc}",
    )(u, bcat, apr, api, ccat, dskip, h0r.reshape(b, 1, N_SSM), h0i.reshape(b, 1, N_SSM), *extra)


def _ssm_params(a_re, a_im, log_dt, b_re, b_im, c_re, c_im, lc):
    a = lax.complex(a_re, a_im)
    adt = a * jnp.exp(log_dt)[:, None]
    a_bar = jnp.exp(adt)
    b_bar = ((a_bar - 1.0) / a)[..., None] * lax.complex(b_re, b_im)
    eye = jnp.eye(N_GROUPS, dtype=F32)
    bre = jnp.einsum("gpc,gh->gchp", jnp.real(b_bar), eye).reshape(D_SSM, N_SSM)
    bim = jnp.einsum("gpc,gh->gchp", jnp.imag(b_bar), eye).reshape(D_SSM, N_SSM)
    bcat = jnp.concatenate([bre, bim], axis=1).astype(BF16)
    cre = jnp.einsum("gcp,gh->gphc", c_re, eye).reshape(N_SSM, D_SSM)
    cim = jnp.einsum("gcp,gh->gphc", c_im, eye).reshape(N_SSM, D_SSM)
    ccat = jnp.concatenate([cre, -cim], axis=0).astype(BF16)
    steps = jnp.arange(1, lc + 1, dtype=F32)[:, None, None]
    apow = jnp.exp(adt[None] * steps).reshape(lc, N_SSM)
    return bcat, jnp.real(apow), jnp.imag(apow), ccat


def _attn_kernel(qc, w, masked, q_ref, k_ref, v_ref, bias_ref, *rest):
    o_ref = rest[-1]
    n = pl.program_id(1)
    q = q_ref[...]
    head = lax.broadcasted_iota(jnp.int32, (qc, D_ATTN), 1) // HEAD_DIM
    qs = jnp.concatenate([jnp.where(head == h, q, jnp.zeros_like(q)) for h in range(N_HEADS)], axis=0)
    start = pl.multiple_of(n * qc, qc)
    k = k_ref[0, pl.ds(start, w), :]
    v = v_ref[0, pl.ds(start, w), :]
    s = lax.dot_general(qs, k, (((1,), (1,)), ((), ())), preferred_element_type=F32)
    s = s * ATTN_SCALE + bias_ref[...]
    if masked:
        col = lax.broadcasted_iota(jnp.int32, (N_HEADS * qc, w), 1)
        s = jnp.where(col + n * qc >= LEFT_CHUNKS * CHUNK, s, jnp.finfo(F32).min)
    m = jnp.max(s, axis=-1, keepdims=True)
    p = jnp.exp(s - m)
    p = p / jnp.sum(p, axis=-1, keepdims=True)
    r = jnp.dot(p.astype(BF16), v, preferred_element_type=F32)
    o = jnp.zeros((qc, D_ATTN), F32)
    for h in range(N_HEADS):
        o = o + jnp.where(head == h, r[h * qc:(h + 1) * qc], 0.0)
    o_ref[...] = o.astype(BF16)


def _attention(q, prev_att, b, l, first_row, k, v, bias, qc, w, masked):
    lk = k.shape[1]
    kv_spec = pl.BlockSpec((1, lk, D_ATTN), lambda i, n: (i, 0, 0))
    rows = _seq_rows_spec(qc, D_ATTN, first_row, l)
    extra, extra_specs, aliases = _alias_prev(prev_att, 4)
    return pl.pallas_call(
        functools.partial(_attn_kernel, qc, w, masked),
        grid=(b, l // qc),
        in_specs=[rows, kv_spec, kv_spec, _const_spec((N_HEADS * qc, w))] + extra_specs,
        out_specs=rows,
        out_shape=jax.ShapeDtypeStruct(q.shape, BF16),
        input_output_aliases=aliases,
        compiler_params=_cparams(2),
        name=f"band_attn_{qc}",
    )(q, k, v, bias, *extra)


def _bias_table(rel_bias, qc, w):
    lo = LEFT_CHUNKS * CHUNK + (qc - 1) - REL_CLIP
    hi = (w + qc - 1) - lo - (2 * REL_CLIP + 1)
    ext = jnp.pad(rel_bias.astype(F32), ((0, 0), (lo, max(hi, 0))), mode="edge")
    rows = [ext[:, qc - 1 - i:qc - 1 - i + w] for i in range(qc)]
    return jnp.stack(rows, axis=1).reshape(N_HEADS * qc, w)


def _mix_kernel(x_ref, y_ref, att_ref, gate_ref, glu_ref, wo_ref, wout_ref, g2_ref, wq_ref,
                x1_ref, xnt_ref, qp_ref):
    glu = jnp.dot(y_ref[...], glu_ref[...], preferred_element_type=F32)
    a = glu[:, :D_MODEL] * jax.nn.sigmoid(glu[:, D_MODEL:])
    b = jnp.dot(att_ref[...], wo_ref[...], preferred_element_type=F32)
    mixed = gate_ref[:, :D_MODEL].astype(F32) * a + gate_ref[:, D_MODEL:].astype(F32) * b
    x1 = x_ref[...] + jnp.dot(mixed.astype(BF16), wout_ref[...], preferred_element_type=F32)
    x1_ref[...] = x1
    ms = jnp.mean(x1 * x1, axis=-1, keepdims=True)
    xn = x1 * lax.rsqrt(ms + RMS_EPS) * g2_ref[...]
    xnt_ref[...] = xn.T.astype(BF16)
    qp_ref[...] = jnp.dot(xn.astype(BF16), wq_ref[...], preferred_element_type=F32).astype(BF16)


def _mix(x, y, att, gates, glu_w, w_o, w_out, g2, w_q):
    t = x.shape[0]
    tm = TOKEN_TILE
    row = lambda n: pl.BlockSpec((tm, n), lambda i: (i, 0))
    dq = PEER_HEADS * PEER_DK
    return pl.pallas_call(
        _mix_kernel,
        grid=(t // tm,),
        in_specs=[row(D_MODEL), row(D_SSM), row(D_ATTN), row(2 * D_MODEL),
                  _const_spec((D_SSM, 2 * D_MODEL)), _const_spec((D_ATTN, D_MODEL)),
                  _const_spec((D_MODEL, D_MODEL)), _const_spec((1, D_MODEL)), _const_spec((D_MODEL, dq))],
        out_specs=[row(D_MODEL), pl.BlockSpec((D_MODEL, tm), lambda i: (0, i)), row(dq)],
        out_shape=[jax.ShapeDtypeStruct((t, D_MODEL), F32), jax.ShapeDtypeStruct((D_MODEL, t), BF16),
                   jax.ShapeDtypeStruct((t, dq), BF16)],
        compiler_params=_cparams(1),
        name="mix",
    )(x, y, att, gates, glu_w, w_o, w_out, g2, w_q)


def _extract_max(s, row):
    m = jnp.max(s, axis=0, keepdims=True)
    pos = jnp.min(jnp.where(s == m, row, float(s.shape[0])), axis=0, keepdims=True)
    return m, pos, jnp.where(row == pos, NEG_INF, s)


def _route_kernel(qp_ref, keys_ref, rows_ref, g_ref, e_ref):
    tm = ROUTE_TILE
    row_k = lax.broadcasted_iota(jnp.int32, (N_KEYS, tm), 0).astype(F32)
    row_c = lax.broadcasted_iota(jnp.int32, (N_CAND_ROWS, tm), 0).astype(F32)
    row_o = lax.broadcasted_iota(jnp.int32, (PEER_TOPK, tm), 0)
    for h in range(PEER_HEADS):
        vals, ids = [], []
        for z in range(2):
            hz = 2 * h + z
            q = qp_ref[:, hz * PEER_HALF:(hz + 1) * PEER_HALF]
            s = lax.dot_general(keys_ref[hz], q, (((1,), (1,)), ((), ())),
                                preferred_element_type=F32)
            v_z, i_z = [], []
            for _ in range(PEER_TOPK):
                m, pos, s = _extract_max(s, row_k)
                v_z.append(m)
                i_z.append(pos)
            vals.append(v_z)
            ids.append(i_z)
        cand = jnp.full((N_CAND_ROWS, tm), NEG_INF, F32)
        cand_e = jnp.zeros((N_CAND_ROWS, tm), F32)
        for c, (i, j) in enumerate(PEER_CAND):
            cand = jnp.where(row_c == float(c), vals[0][i] + vals[1][j], cand)
            cand_e = jnp.where(row_c == float(c), ids[0][i] * float(N_KEYS) + ids[1][j], cand_e)
        best = jnp.zeros((PEER_TOPK, tm), F32)
        best_e = jnp.zeros((PEER_TOPK, tm), F32)
        top = None
        for k in range(PEER_TOPK):
            m, pos, cand = _extract_max(cand, row_c)
            e = jnp.max(jnp.where(row_c == pos, cand_e, -1.0), axis=0, keepdims=True)
            top = m if top is None else top
            best = jnp.where(row_o == k, jnp.exp(m - top), best)
            best_e = jnp.where(row_o == k, e, best_e)
        e_ref[h * PEER_TOPK:(h + 1) * PEER_TOPK, :] = best_e
        g_ref[h * PEER_TOPK:(h + 1) * PEER_TOPK, :] = best / jnp.sum(best, axis=0, keepdims=True)
    rows_ref[...] = e_ref[...].T.astype(jnp.int32) * ROW_WORDS


def _route(qp, keys_bf):
    t = qp.shape[0]
    tm = ROUTE_TILE
    out = pl.BlockSpec((PEER_SLOTS, tm), lambda i: (0, i))
    return pl.pallas_call(
        _route_kernel,
        grid=(t // tm,),
        in_specs=[pl.BlockSpec((tm, PEER_HEADS * PEER_DK), lambda i: (i, 0)),
                  _const_spec((2 * PEER_HEADS, N_KEYS, PEER_HALF))],
        out_specs=[pl.BlockSpec((tm, PEER_SLOTS), lambda i: (i, 0)), out],
        out_shape=[jax.ShapeDtypeStruct((t, PEER_SLOTS), jnp.int32),
                   jax.ShapeDtypeStruct((PEER_SLOTS, t), F32)],
        scratch_shapes=[pltpu.VMEM((PEER_SLOTS, tm), F32)],
        compiler_params=_cparams(1),
        name="peer_route",
    )(qp, keys_bf)


def _pack_table(tab):
    tb = tab.astype(BF16)
    hi = lax.bitcast_convert_type(tb[:, :512], jnp.uint16).astype(jnp.uint32)
    lo = lax.bitcast_convert_type(tb[:, 512:], jnp.uint16).astype(jnp.uint32)
    return ((hi << 16) | lo).reshape(N_EXPERTS * ROW_WORDS, 128)


def _gather_group(idx_ref, g, tab_ref, bufs):
    for u, buf in enumerate(bufs):
        for a in range(PEER_SLOTS // IDX_QUAD):
            quad = idx_ref.at[g * PEER_UNROLL + u, pl.ds(a * IDX_QUAD, IDX_QUAD)]
            for b in range(IDX_QUAD):
                k = a * IDX_QUAD + b
                row = pl.multiple_of(quad[b], ROW_WORDS)
                buf[pl.ds(k, ROW_WORDS, stride=G_STRIDE), :] = tab_ref[pl.ds(row, ROW_WORDS), :]


def _lane_tile(buf_ref, j):
    word = buf_ref[j * G_STRIDE:j * G_STRIDE + PEER_SLOTS, :]
    hi = pltpu.bitcast(word & jnp.uint32(0xFFFF0000), F32)
    lo = pltpu.bitcast(word << 16, F32)
    return hi, lo


def _token_pipeline(idx_ref, next_idx_ref, tab_ref, bufs, compute_group, split_regions):
    n_groups = PEER_TOKENS // PEER_UNROLL
    halves = (bufs[:PEER_UNROLL], bufs[PEER_UNROLL:])
    step = pl.program_id(0)

    @pl.when(step == 0)
    def _():
        _gather_group(idx_ref, 0, tab_ref, halves[0])

    def pair(g, half, src_ref, src_group):
        def run():
            _gather_group(src_ref, src_group, tab_ref, halves[1 - half])
            compute_group(g * PEER_UNROLL, halves[half])

        if split_regions:
            pl.when(step >= 0)(run)
        else:
            run()

    def body(i, _):
        for half in range(2):
            pair(2 * i + half, half, idx_ref, 2 * i + half + 1)
        return 0

    lax.fori_loop(0, n_groups // 2 - 1, body, 0)
    pair(n_groups - 2, 0, idx_ref, n_groups - 1)
    pair(n_groups - 1, 1, next_idx_ref, 0)


def _gathered_rows(buf):
    tiles = [_lane_tile(buf, j) for j in range(ROW_WORDS)]
    return jnp.concatenate([tl[0] for tl in tiles] + [tl[1] for tl in tiles], axis=1)


def _peer_u_kernel(idx_ref, next_idx_ref, tab_ref, xt_ref, g_ref, w_ref, xw_ref, act_ref, *bufs):
    lane = lax.broadcasted_iota(jnp.int32, (PEER_SLOTS, PEER_TOKENS), 1)
    xw_ref[...] = xt_ref[...].astype(F32)

    def compute_group(t0, bufs):
        lhs = jnp.concatenate([_gathered_rows(b) for b in bufs], axis=0)
        r = jnp.dot(lhs, xw_ref[...], preferred_element_type=F32)
        for u in range(PEER_UNROLL):
            pltpu.store(act_ref, r[u * PEER_SLOTS:(u + 1) * PEER_SLOTS], mask=lane == t0 + u)

    _token_pipeline(idx_ref, next_idx_ref, tab_ref, bufs, compute_group, split_regions=True)
    act = act_ref[...]
    gelu = 0.5 * act * (1.0 + lax.erf(act * (1.0 / math.sqrt(2.0))))
    w_ref[...] = g_ref[...] * gelu


def _peer_v_kernel(idx_ref, next_idx_ref, tab_ref, wgt_ref, x_ref, o_ref, wt_ref, *bufs):
    sub = lax.broadcasted_iota(jnp.int32, (8, PEER_SLOTS), 0)
    wt_ref[...] = wgt_ref[...].T

    def compute_group(t0, bufs):
        for u, buf in enumerate(bufs):
            w = wt_ref[pl.ds(t0 + u, 1), :]
            w0 = w.astype(BF16).astype(F32)
            r1 = w - w0
            w1 = r1.astype(BF16).astype(F32)
            w2 = r1 - w1
            lhs = jnp.where(sub == 0, w0, jnp.where(sub == 1, w1, jnp.where(sub == 2, w2, 0.0)))
            r = jnp.dot(lhs, _gathered_rows(buf), preferred_element_type=F32)
            o_ref[pl.ds(t0 + u, 1), :] = x_ref[pl.ds(t0 + u, 1), :] + (r[0:1] + r[1:2] + r[2:3])

    _token_pipeline(idx_ref, next_idx_ref, tab_ref, bufs, compute_group, split_regions=False)


def _idx_specs(n_steps):
    shape = (PEER_TOKENS, PEER_SLOTS)
    return [pl.BlockSpec(shape, lambda i: (i, 0), memory_space=pltpu.SMEM),
            pl.BlockSpec(shape, lambda i: (jnp.minimum(i + 1, n_steps - 1), 0), memory_space=pltpu.SMEM)]


def _table_spec():
    return pl.BlockSpec((N_EXPERTS * ROW_WORDS, 128), lambda i: (0, 0), pipeline_mode=pl.Buffered(1))


def _gather_buffers():
    return [pltpu.VMEM((G_ROWS, 128), jnp.uint32) for _ in range(2 * PEER_UNROLL)]


def _peer_u(rows, tab, xt, g_t):
    t = xt.shape[1]
    n_steps = t // PEER_TOKENS
    col = pl.BlockSpec((PEER_SLOTS, PEER_TOKENS), lambda i: (0, i))
    return pl.pallas_call(
        _peer_u_kernel,
        grid=(n_steps,),
        in_specs=_idx_specs(n_steps) + [_table_spec(),
                                        pl.BlockSpec((D_MODEL, PEER_TOKENS), lambda i: (0, i)), col],
        out_specs=col,
        out_shape=jax.ShapeDtypeStruct((PEER_SLOTS, t), F32),
        scratch_shapes=[pltpu.VMEM((D_MODEL, PEER_TOKENS), F32),
                        pltpu.VMEM((PEER_SLOTS, PEER_TOKENS), F32)] + _gather_buffers(),
        compiler_params=_cparams(1),
        name="peer_u",
    )(rows, rows, tab, xt, g_t)


def _peer_v(rows, tab, wgt_t, x):
    t = x.shape[0]
    n_steps = t // PEER_TOKENS
    row = pl.BlockSpec((PEER_TOKENS, D_MODEL), lambda i: (i, 0))
    col = pl.BlockSpec((PEER_SLOTS, PEER_TOKENS), lambda i: (0, i))
    return pl.pallas_call(
        _peer_v_kernel,
        grid=(n_steps,),
        in_specs=_idx_specs(n_steps) + [_table_spec(), col, row],
        out_specs=row,
        out_shape=jax.ShapeDtypeStruct((t, D_MODEL), F32),
        scratch_shapes=[pltpu.VMEM((PEER_TOKENS, PEER_SLOTS), F32)] + _gather_buffers(),
        compiler_params=_cparams(1),
        name="peer_v",
    )(rows, rows, tab, wgt_t, x)


def _final_norm_kernel(x_ref, g_ref, y_ref):
    x = x_ref[...]
    ms = jnp.mean(x * x, axis=-1, keepdims=True)
    y_ref[...] = x * lax.rsqrt(ms + RMS_EPS) * g_ref[...]


def _final_norm(x, g):
    t = x.shape[0]
    tm = TOKEN_TILE
    return pl.pallas_call(
        _final_norm_kernel,
        grid=(t // tm,),
        in_specs=[pl.BlockSpec((tm, D_MODEL), lambda i: (i, 0)), _const_spec((1, D_MODEL))],
        out_specs=pl.BlockSpec((tm, D_MODEL), lambda i: (i, 0)),
        out_shape=jax.ShapeDtypeStruct((t, D_MODEL), F32),
        compiler_params=_cparams(1),
        name="final_norm",
    )(x, g)


def kernel(x_prompt, x_sample, cache_k, cache_v, state_ssm_re, state_ssm_im, norm1_g, w_in, ssm_a_re, ssm_a_im, ssm_log_dt, ssm_b_re, ssm_b_im, ssm_c_re, ssm_c_im, ssm_d, ssm_glu_w, attn_rel_bias, attn_w_o, w_out, norm2_g, peer_w_q, peer_sub_keys, peer_u, peer_v, final_g):
    bp, lp, _ = x_prompt.shape
    bs, ls, _ = x_sample.shape
    depth = w_in.shape[0]
    tp = bp * lp
    kv_win = cache_k.shape[2]
    keep = min(LEFT_CHUNKS * CHUNK, lp)
    x = jnp.concatenate([x_prompt.reshape(tp, D_MODEL), x_sample.reshape(bs * ls, D_MODEL)], axis=0)
    t = x.shape[0]
    assert t % TOKEN_TILE == 0 and lp % SCAN_CHUNK == 0 and lp % CHUNK == 0 and kv_win == LEFT_CHUNKS * CHUNK

    zeros_state = jnp.zeros((bp, N_SSM), F32)
    outs = {n: [] for n in ("p_re", "p_im", "p_k", "p_v", "s_re", "s_im", "s_k", "s_v")}
    for l in range(depth):
        u, q, kv, kvb, gates = _inproj(x, norm1_g[l][None], w_in[l].astype(BF16))

        ssm_args = (ssm_a_re[l], ssm_a_im[l], ssm_log_dt[l], ssm_b_re[l], ssm_b_im[l], ssm_c_re[l], ssm_c_im[l])
        dskip = ssm_d[l][None]
        bcat, apr, api, ccat = _ssm_params(*ssm_args, SCAN_CHUNK)
        y, hr_p, hi_p = _ssm(u, None, bp, lp, 0, zeros_state, zeros_state,
                             bcat, apr, api, ccat, dskip, SCAN_CHUNK)
        y, hr_s, hi_s = _ssm(u, y, bs, ls, tp, state_ssm_re[l].reshape(bs, N_SSM),
                             state_ssm_im[l].reshape(bs, N_SSM), bcat, apr[:ls], api[:ls], ccat, dskip, ls)

        pad = ((0, 0), (LEFT_CHUNKS * CHUNK, 0), (0, 0))
        kb_p = jnp.pad(kvb[:tp, :D_ATTN].reshape(bp, lp, D_ATTN), pad)
        vb_p = jnp.pad(kvb[:tp, D_ATTN:].reshape(bp, lp, D_ATTN), pad)
        w_p = (LEFT_CHUNKS + 1) * CHUNK
        att = _attention(q, None, bp, lp, 0, kb_p, vb_p,
                         _bias_table(attn_rel_bias[l], CHUNK, w_p), CHUNK, w_p, True)
        kb_s = jnp.concatenate([cache_k[l].reshape(bs, kv_win, D_ATTN).astype(BF16),
                                kvb[tp:, :D_ATTN].reshape(bs, ls, D_ATTN)], axis=1)
        vb_s = jnp.concatenate([cache_v[l].reshape(bs, kv_win, D_ATTN).astype(BF16),
                                kvb[tp:, D_ATTN:].reshape(bs, ls, D_ATTN)], axis=1)
        w_s = kv_win + ls
        att = _attention(q, att, bs, ls, tp, kb_s, vb_s,
                         _bias_table(attn_rel_bias[l], ls, w_s), ls, w_s, False)

        x1, xn_t, qp = _mix(x, y, att, gates, ssm_glu_w[l].astype(BF16), attn_w_o[l].astype(BF16),
                          w_out[l].astype(BF16), norm2_g[l][None], peer_w_q[l].astype(BF16))
        keys = peer_sub_keys[l].reshape(2 * PEER_HEADS, N_KEYS, PEER_HALF).astype(BF16)
        rows, g_t = _route(qp, keys)
        wgt_t = _peer_u(rows, _pack_table(peer_u[l]), xn_t, g_t)
        x = _peer_v(rows, _pack_table(peer_v[l]), wgt_t, x1)

        kv_p = kv[:tp].reshape(bp, lp, 2, N_HEADS, HEAD_DIM)[:, lp - keep:]
        kv_s = kv[tp:].reshape(bs, ls, 2, N_HEADS, HEAD_DIM)
        outs["p_re"].append(hr_p.reshape(bp, N_GROUPS, SSM_STATE))
        outs["p_im"].append(hi_p.reshape(bp, N_GROUPS, SSM_STATE))
        outs["p_k"].append(kv_p[:, :, 0])
        outs["p_v"].append(kv_p[:, :, 1])
        outs["s_re"].append(hr_s.reshape(bs, N_GROUPS, SSM_STATE))
        outs["s_im"].append(hi_s.reshape(bs, N_GROUPS, SSM_STATE))
        outs["s_k"].append(kv_s[:, :, 0])
        outs["s_v"].append(kv_s[:, :, 1])

    y = _final_norm(x, final_g[None])
    st = {n: jnp.stack(v) for n, v in outs.items()}
    return (y[:tp].reshape(bp, lp, D_MODEL), y[tp:].reshape(bs, ls, D_MODEL),
            st["p_re"], st["p_im"], st["p_k"], st["p_v"], st["s_re"], st["s_im"], st["s_k"], st["s_v"])
```

```python
import functools
import math

import jax
import jax.numpy as jnp
import numpy as np
from jax import lax
from jax.experimental import pallas as pl
from jax.experimental.pallas import tpu as pltpu

F32 = jnp.float32
BF16 = jnp.bfloat16

D_MODEL = 1024
CHUNK = 64
D_SSM = 512
SSM_GROUP = 16
N_GROUPS = D_SSM // SSM_GROUP
SSM_STATE = 64
N_SSM = N_GROUPS * SSM_STATE
N_HEADS = 8
HEAD_DIM = 64
D_ATTN = N_HEADS * HEAD_DIM
LEFT_CHUNKS = 8
REL_CLIP = 128
ATTN_SCALE = HEAD_DIM ** -0.5
N_KEYS = 128
N_EXPERTS = N_KEYS * N_KEYS
PEER_HEADS = 8
PEER_DK = 256
PEER_HALF = PEER_DK // 2
PEER_TOPK = 16
PEER_SLOTS = PEER_HEADS * PEER_TOPK
RMS_EPS = 1e-6
D_IN = D_SSM + 3 * D_ATTN + 2 * D_MODEL

VMEM_LIMIT_BYTES = 56 * 1024 * 1024
TOKEN_TILE = 256
ROUTE_TILE = 128
PEER_TOKENS = 128
PEER_UNROLL = 8
SCAN_CHUNK = 256
SCAN_ROWS = 8
NEG_INF = float("-inf")

ROW_WORDS = 4
G_STRIDE = 136
G_ROWS = ROW_WORDS * G_STRIDE
IDX_QUAD = 4

PEER_CAND = [(i, j) for i in range(PEER_TOPK) for j in range(PEER_TOPK) if (i + 1) * (j + 1) <= PEER_TOPK]
N_CAND_ROWS = 56


def _cparams(n_axes):
    return pltpu.CompilerParams(dimension_semantics=("arbitrary",) * n_axes,
                                vmem_limit_bytes=VMEM_LIMIT_BYTES)


def _const_spec(shape):
    return pl.BlockSpec(shape, lambda *_: (0,) * len(shape))


def _inproj_kernel(x_ref, g_ref, w_ref, u_ref, q_ref, kv_ref, kvb_ref, gate_ref):
    x = x_ref[...]
    ms = jnp.mean(x * x, axis=-1, keepdims=True)
    xn = (x * lax.rsqrt(ms + RMS_EPS) * g_ref[...]).astype(BF16)

    def proj(lo, hi):
        return jnp.dot(xn, w_ref[:, lo:hi], preferred_element_type=F32)

    u_ref[...] = proj(0, D_SSM).astype(BF16)
    q_ref[...] = proj(D_SSM, D_SSM + D_ATTN).astype(BF16)
    for c in range(2):
        lo = D_SSM + D_ATTN + c * D_ATTN
        kv = proj(lo, lo + D_ATTN)
        kv_ref[:, c * D_ATTN:(c + 1) * D_ATTN] = kv
        kvb_ref[:, c * D_ATTN:(c + 1) * D_ATTN] = kv.astype(BF16)
    for c in range(4):
        lo = D_SSM + 3 * D_ATTN + c * 512
        gate_ref[:, c * 512:(c + 1) * 512] = jax.nn.sigmoid(proj(lo, lo + 512)).astype(BF16)


def _inproj(x, g, w_bf):
    t = x.shape[0]
    tm = TOKEN_TILE
    row = lambda n: pl.BlockSpec((tm, n), lambda i: (i, 0))
    return pl.pallas_call(
        _inproj_kernel,
        grid=(t // tm,),
        in_specs=[row(D_MODEL), _const_spec((1, D_MODEL)), _const_spec((D_MODEL, D_IN))],
        out_specs=[row(D_SSM), row(D_ATTN), row(2 * D_ATTN), row(2 * D_ATTN), row(2 * D_MODEL)],
        out_shape=[jax.ShapeDtypeStruct((t, D_SSM), BF16), jax.ShapeDtypeStruct((t, D_ATTN), BF16),
                   jax.ShapeDtypeStruct((t, 2 * D_ATTN), F32), jax.ShapeDtypeStruct((t, 2 * D_ATTN), BF16),
                   jax.ShapeDtypeStruct((t, 2 * D_MODEL), BF16)],
        compiler_params=_cparams(1),
        name="inproj",
    )(x, g, w_bf)


def _ssm_kernel(lc, u_ref, bcat_ref, apr_ref, api_ref, ccat_ref, d_ref, h0r_ref, h0i_ref, *rest):
    y_ref, hr_ref, hi_ref, re_ref, im_ref, cr_ref, ci_ref = rest[-7:]
    c = pl.program_id(1)

    @pl.when(c == 0)
    def _():
        cr_ref[...] = h0r_ref[0]
        ci_ref[...] = h0i_ref[0]

    u = u_ref[...]
    bu = jnp.dot(u, bcat_ref[...], preferred_element_type=F32)
    re_ref[...] = bu[:, :N_SSM]
    im_ref[...] = bu[:, N_SSM:]

    row = lax.broadcasted_iota(jnp.int32, (SCAN_ROWS, 128), 0)

    def lane_tile(j, _):
        sl = pl.ds(pl.multiple_of(j * 128, 128), 128)
        steps = []
        s = 1
        while s < SCAN_ROWS:
            steps.append((s, apr_ref[s - 1:s, sl], api_ref[s - 1:s, sl], row >= s))
            s *= 2
        pr = apr_ref[0:SCAN_ROWS, sl]
        pi = api_ref[0:SCAN_ROWS, sl]
        c_r = cr_ref[:, sl]
        c_i = ci_ref[:, sl]
        for blk in range(lc // SCAN_ROWS):
            rows = slice(blk * SCAN_ROWS, (blk + 1) * SCAN_ROWS)
            hr = re_ref[rows, sl]
            hi = im_ref[rows, sl]
            for s, ar, ai, keep in steps:
                sr = jnp.where(keep, pltpu.roll(hr, s, 0), 0.0)
                si = jnp.where(keep, pltpu.roll(hi, s, 0), 0.0)
                hr, hi = hr + (ar * sr - ai * si), hi + (ar * si + ai * sr)
            hr, hi = hr + (pr * c_r - pi * c_i), hi + (pr * c_i + pi * c_r)
            re_ref[rows, sl] = hr
            im_ref[rows, sl] = hi
            c_r = hr[SCAN_ROWS - 1:SCAN_ROWS]
            c_i = hi[SCAN_ROWS - 1:SCAN_ROWS]
        cr_ref[:, sl] = c_r
        ci_ref[:, sl] = c_i
        return 0

    lax.fori_loop(0, N_SSM // 128, lane_tile, 0)

    hr_ref[0] = cr_ref[...]
    hi_ref[0] = ci_ref[...]
    y = jnp.dot(re_ref[...].astype(BF16), ccat_ref[:N_SSM], preferred_element_type=F32)
    y = y + jnp.dot(im_ref[...].astype(BF16), ccat_ref[N_SSM:], preferred_element_type=F32)
    y_ref[...] = (y + d_ref[...] * u.astype(F32)).astype(BF16)


def _seq_rows_spec(rows, width, first_row, seq_len):
    per_seq = seq_len // rows
    base = first_row // rows
    return pl.BlockSpec((rows, width), lambda i, n: (base + i * per_seq + n, 0))


def _alias_prev(prev, n_inputs):
    if prev is None:
        return [], [], {}
    return [prev], [pl.BlockSpec(memory_space=pl.ANY)], {n_inputs: 0}


def _ssm(u, prev_y, b, l, first_row, h0r, h0i, bcat, apr, api, ccat, dskip, lc):
    st = jax.ShapeDtypeStruct((b, 1, N_SSM), F32)
    state_spec = pl.BlockSpec((1, 1, N_SSM), lambda i, c: (i, 0, 0))
    rows = _seq_rows_spec(lc, D_SSM, first_row, l)
    extra, extra_specs, aliases = _alias_prev(prev_y, 8)
    return pl.pallas_call(
        functools.partial(_ssm_kernel, lc),
        grid=(b, l // lc),
        in_specs=[rows, _const_spec((D_SSM, 2 * N_SSM)), _const_spec((lc, N_SSM)), _const_spec((lc, N_SSM)),
                  _const_spec((2 * N_SSM, D_SSM)), _const_spec((1, D_SSM)), state_spec, state_spec] + extra_specs,
        out_specs=[rows, state_spec, state_spec],
        out_shape=[jax.ShapeDtypeStruct(u.shape, BF16), st, st],
        scratch_shapes=[pltpu.VMEM((lc, N_SSM), F32), pltpu.VMEM((lc, N_SSM), F32),
                        pltpu.VMEM((1, N_SSM), F32), pltpu.VMEM((1, N_SSM), F32)],
        input_output_aliases=aliases,
        compiler_params=_cparams(2),
        name=f"ssm_scan_{lc}",
    )(u, bcat, apr, api, ccat, dskip, h0r.reshape(b, 1, N_SSM), h0i.reshape(b, 1, N_SSM), *extra)


def _ssm_params(a_re, a_im, log_dt, b_re, b_im, c_re, c_im, lc):
    a = lax.complex(a_re, a_im)
    adt = a * jnp.exp(log_dt)[:, None]
    a_bar = jnp.exp(adt)
    b_bar = ((a_bar - 1.0) / a)[..., None] * lax.complex(b_re, b_im)
    eye = jnp.eye(N_GROUPS, dtype=F32)
    bre = jnp.einsum("gpc,gh->gchp", jnp.real(b_bar), eye).reshape(D_SSM, N_SSM)
    bim = jnp.einsum("gpc,gh->gchp", jnp.imag(b_bar), eye).reshape(D_SSM, N_SSM)
    bcat = jnp.concatenate([bre, bim], axis=1).astype(BF16)
    cre = jnp.einsum("gcp,gh->gphc", c_re, eye).reshape(N_SSM, D_SSM)
    cim = jnp.einsum("gcp,gh->gphc", c_im, eye).reshape(N_SSM, D_SSM)
    ccat = jnp.concatenate([cre, -cim], axis=0).astype(BF16)
    steps = jnp.arange(1, lc + 1, dtype=F32)[:, None, None]
    apow = jnp.exp(adt[None] * steps).reshape(lc, N_SSM)
    return bcat, jnp.real(apow), jnp.imag(apow), ccat


def _attn_kernel(qc, w, masked, q_ref, k_ref, v_ref, bias_ref, *rest):
    o_ref = rest[-1]
    n = pl.program_id(1)
    second = lax.broadcasted_iota(jnp.int32, (qc, 2 * HEAD_DIM), 1) >= HEAD_DIM
    start = pl.multiple_of(n * qc, qc)
    scores = []
    for pr in range(N_HEADS // 2):
        lanes = slice(pr * 2 * HEAD_DIM, (pr + 1) * 2 * HEAD_DIM)
        q2 = q_ref[:, lanes]
        zero = jnp.zeros_like(q2)
        qs = jnp.concatenate([jnp.where(second, zero, q2), jnp.where(second, q2, zero)], axis=0)
        k2 = k_ref[0, pl.ds(start, w), lanes]
        scores.append(lax.dot_general(qs, k2, (((1,), (1,)), ((), ())), preferred_element_type=F32))
    s = jnp.concatenate(scores, axis=0)
    s = s * ATTN_SCALE + bias_ref[...]
    if masked:
        col = lax.broadcasted_iota(jnp.int32, (N_HEADS * qc, w), 1)
        s = jnp.where(col + n * qc >= LEFT_CHUNKS * CHUNK, s, jnp.finfo(F32).min)
    m = jnp.max(s, axis=-1, keepdims=True)
    p = jnp.exp(s - m)
    p = p / jnp.sum(p, axis=-1, keepdims=True)
    p = p.astype(BF16)
    for pr in range(N_HEADS // 2):
        lanes = slice(pr * 2 * HEAD_DIM, (pr + 1) * 2 * HEAD_DIM)
        v2 = v_ref[0, pl.ds(start, w), lanes]
        r = jnp.dot(p[2 * pr * qc:(2 * pr + 2) * qc], v2, preferred_element_type=F32)
        o_ref[:, lanes] = jnp.where(second, r[qc:], r[:qc]).astype(BF16)


def _attention(q, prev_att, b, l, first_row, k, v, bias, qc, w, masked):
    lk = k.shape[1]
    kv_spec = pl.BlockSpec((1, lk, D_ATTN), lambda i, n: (i, 0, 0))
    rows = _seq_rows_spec(qc, D_ATTN, first_row, l)
    extra, extra_specs, aliases = _alias_prev(prev_att, 4)
    return pl.pallas_call(
        functools.partial(_attn_kernel, qc, w, masked),
        grid=(b, l // qc),
        in_specs=[rows, kv_spec, kv_spec, _const_spec((N_HEADS * qc, w))] + extra_specs,
        out_specs=rows,
        out_shape=jax.ShapeDtypeStruct(q.shape, BF16),
        input_output_aliases=aliases,
        compiler_params=_cparams(2),
        name=f"band_attn_{qc}",
    )(q, k, v, bias, *extra)


def _bias_table(rel_bias, qc, w):
    lo = LEFT_CHUNKS * CHUNK + (qc - 1) - REL_CLIP
    hi = (w + qc - 1) - lo - (2 * REL_CLIP + 1)
    ext = jnp.pad(rel_bias.astype(F32), ((0, 0), (lo, max(hi, 0))), mode="edge")
    rows = [ext[:, qc - 1 - i:qc - 1 - i + w] for i in range(qc)]
    return jnp.stack(rows, axis=1).reshape(N_HEADS * qc, w)


def _mix_kernel(x_ref, y_ref, att_ref, gate_ref, glu_ref, wo_ref, wout_ref, g2_ref, wq_ref,
                x1_ref, xnt_ref, qp_ref):
    glu = jnp.dot(y_ref[...], glu_ref[...], preferred_element_type=F32)
    a = glu[:, :D_MODEL] * jax.nn.sigmoid(glu[:, D_MODEL:])
    b = jnp.dot(att_ref[...], wo_ref[...], preferred_element_type=F32)
    mixed = gate_ref[:, :D_MODEL].astype(F32) * a + gate_ref[:, D_MODEL:].astype(F32) * b
    x1 = x_ref[...] + jnp.dot(mixed.astype(BF16), wout_ref[...], preferred_element_type=F32)
    x1_ref[...] = x1
    ms = jnp.mean(x1 * x1, axis=-1, keepdims=True)
    xn = x1 * lax.rsqrt(ms + RMS_EPS) * g2_ref[...]
    xnt_ref[...] = xn.T.astype(BF16)
    qp_ref[...] = jnp.dot(xn.astype(BF16), wq_ref[...], preferred_element_type=F32).astype(BF16)


def _mix(x, y, att, gates, glu_w, w_o, w_out, g2, w_q):
    t = x.shape[0]
    tm = TOKEN_TILE
    row = lambda n: pl.BlockSpec((tm, n), lambda i: (i, 0))
    dq = PEER_HEADS * PEER_DK
    return pl.pallas_call(
        _mix_kernel,
        grid=(t // tm,),
        in_specs=[row(D_MODEL), row(D_SSM), row(D_ATTN), row(2 * D_MODEL),
                  _const_spec((D_SSM, 2 * D_MODEL)), _const_spec((D_ATTN, D_MODEL)),
                  _const_spec((D_MODEL, D_MODEL)), _const_spec((1, D_MODEL)), _const_spec((D_MODEL, dq))],
        out_specs=[row(D_MODEL), pl.BlockSpec((D_MODEL, tm), lambda i: (0, i)), row(dq)],
        out_shape=[jax.ShapeDtypeStruct((t, D_MODEL), F32), jax.ShapeDtypeStruct((D_MODEL, t), BF16),
                   jax.ShapeDtypeStruct((t, dq), BF16)],
        compiler_params=_cparams(1),
        name="mix",
    )(x, y, att, gates, glu_w, w_o, w_out, g2, w_q)


def _extract_max(s, row):
    m = jnp.max(s, axis=0, keepdims=True)
    pos = jnp.min(jnp.where(s == m, row, float(s.shape[0])), axis=0, keepdims=True)
    return m, pos, jnp.where(row == pos, NEG_INF, s)


def _route_kernel(qp_ref, keys_ref, rows_ref, g_ref, e_ref):
    tm = ROUTE_TILE
    row_k = lax.broadcasted_iota(jnp.int32, (N_KEYS, tm), 0).astype(F32)
    row_c = lax.broadcasted_iota(jnp.int32, (N_CAND_ROWS, tm), 0).astype(F32)
    row_o = lax.broadcasted_iota(jnp.int32, (PEER_TOPK, tm), 0)
    for h in range(PEER_HEADS):
        vals, ids = [], []
        for z in range(2):
            hz = 2 * h + z
            q = qp_ref[:, hz * PEER_HALF:(hz + 1) * PEER_HALF]
            s = lax.dot_general(keys_ref[hz], q, (((1,), (1,)), ((), ())),
                                preferred_element_type=F32)
            v_z, i_z = [], []
            for _ in range(PEER_TOPK):
                m, pos, s = _extract_max(s, row_k)
                v_z.append(m)
                i_z.append(pos)
            vals.append(v_z)
            ids.append(i_z)
        cand = jnp.full((N_CAND_ROWS, tm), NEG_INF, F32)
        cand_e = jnp.zeros((N_CAND_ROWS, tm), F32)
        for c, (i, j) in enumerate(PEER_CAND):
            cand = jnp.where(row_c == float(c), vals[0][i] + vals[1][j], cand)
            cand_e = jnp.where(row_c == float(c), ids[0][i] * float(N_KEYS) + ids[1][j], cand_e)
        best = jnp.zeros((PEER_TOPK, tm), F32)
        best_e = jnp.zeros((PEER_TOPK, tm), F32)
        top = None
        for k in range(PEER_TOPK):
            m, pos, cand = _extract_max(cand, row_c)
            e = jnp.max(jnp.where(row_c == pos, cand_e, -1.0), axis=0, keepdims=True)
            top = m if top is None else top
            best = jnp.where(row_o == k, jnp.exp(m - top), best)
            best_e = jnp.where(row_o == k, e, best_e)
        e_ref[h * PEER_TOPK:(h + 1) * PEER_TOPK, :] = best_e
        g_ref[h * PEER_TOPK:(h + 1) * PEER_TOPK, :] = best / jnp.sum(best, axis=0, keepdims=True)
    rows_ref[...] = e_ref[...].T.astype(jnp.int32) * ROW_WORDS


def _route(qp, keys_bf):
    t = qp.shape[0]
    tm = ROUTE_TILE
    out = pl.BlockSpec((PEER_SLOTS, tm), lambda i: (0, i))
    return pl.pallas_call(
        _route_kernel,
        grid=(t // tm,),
        in_specs=[pl.BlockSpec((tm, PEER_HEADS * PEER_DK), lambda i: (i, 0)),
                  _const_spec((2 * PEER_HEADS, N_KEYS, PEER_HALF))],
        out_specs=[pl.BlockSpec((tm, PEER_SLOTS), lambda i: (i, 0)), out],
        out_shape=[jax.ShapeDtypeStruct((t, PEER_SLOTS), jnp.int32),
                   jax.ShapeDtypeStruct((PEER_SLOTS, t), F32)],
        scratch_shapes=[pltpu.VMEM((PEER_SLOTS, tm), F32)],
        compiler_params=_cparams(1),
        name="peer_route",
    )(qp, keys_bf)


def _pack_table(tab):
    tb = tab.astype(BF16)
    hi = lax.bitcast_convert_type(tb[:, :512], jnp.uint16).astype(jnp.uint32)
    lo = lax.bitcast_convert_type(tb[:, 512:], jnp.uint16).astype(jnp.uint32)
    return ((hi << 16) | lo).reshape(N_EXPERTS * ROW_WORDS, 128)


def _gather_group(idx_ref, g, tab_ref, bufs):
    for u, buf in enumerate(bufs):
        for a in range(PEER_SLOTS // IDX_QUAD):
            quad = idx_ref.at[g * PEER_UNROLL + u, pl.ds(a * IDX_QUAD, IDX_QUAD)]
            for b in range(IDX_QUAD):
                k = a * IDX_QUAD + b
                row = pl.multiple_of(quad[b], ROW_WORDS)
                buf[pl.ds(k, ROW_WORDS, stride=G_STRIDE), :] = tab_ref[pl.ds(row, ROW_WORDS), :]


def _lane_tile(buf_ref, j):
    word = buf_ref[j * G_STRIDE:j * G_STRIDE + PEER_SLOTS, :]
    hi = pltpu.bitcast(word & jnp.uint32(0xFFFF0000), F32)
    lo = pltpu.bitcast(word << 16, F32)
    return hi, lo


def _token_pipeline(idx_ref, next_idx_ref, tab_ref, bufs, compute_group, split_regions):
    n_groups = PEER_TOKENS // PEER_UNROLL
    halves = (bufs[:PEER_UNROLL], bufs[PEER_UNROLL:])
    step = pl.program_id(0)

    @pl.when(step == 0)
    def _():
        _gather_group(idx_ref, 0, tab_ref, halves[0])

    def pair(g, half, src_ref, src_group):
        def run():
            _gather_group(src_ref, src_group, tab_ref, halves[1 - half])
            compute_group(g * PEER_UNROLL, halves[half])

        if split_regions:
            pl.when(step >= 0)(run)
        else:
            run()

    def body(i, _):
        for half in range(2):
            pair(2 * i + half, half, idx_ref, 2 * i + half + 1)
        return 0

    lax.fori_loop(0, n_groups // 2 - 1, body, 0)
    pair(n_groups - 2, 0, idx_ref, n_groups - 1)
    pair(n_groups - 1, 1, next_idx_ref, 0)


def _gathered_rows(buf):
    tiles = [_lane_tile(buf, j) for j in range(ROW_WORDS)]
    return jnp.concatenate([tl[0] for tl in tiles] + [tl[1] for tl in tiles], axis=1)


def _peer_u_kernel(idx_ref, next_idx_ref, tab_ref, xt_ref, g_ref, w_ref, xw_ref, act_ref, *bufs):
    lane = lax.broadcasted_iota(jnp.int32, (PEER_SLOTS, PEER_TOKENS), 1)
    xw_ref[...] = xt_ref[...].astype(F32)

    def compute_group(t0, bufs):
        lhs = jnp.concatenate([_gathered_rows(b) for b in bufs], axis=0)
        r = jnp.dot(lhs, xw_ref[...], preferred_element_type=F32)
        for u in range(PEER_UNROLL):
            pltpu.store(act_ref, r[u * PEER_SLOTS:(u + 1) * PEER_SLOTS], mask=lane == t0 + u)

    _token_pipeline(idx_ref, next_idx_ref, tab_ref, bufs, compute_group, split_regions=True)
    act = act_ref[...]
    gelu = 0.5 * act * (1.0 + lax.erf(act * (1.0 / math.sqrt(2.0))))
    w_ref[...] = g_ref[...] * gelu


def _peer_v_kernel(idx_ref, next_idx_ref, tab_ref, wgt_ref, x_ref, o_ref, wt_ref, *bufs):
    sub = lax.broadcasted_iota(jnp.int32, (8, PEER_SLOTS), 0)
    wt_ref[...] = wgt_ref[...].T

    def compute_group(t0, bufs):
        for u, buf in enumerate(bufs):
            w = wt_ref[pl.ds(t0 + u, 1), :]
            w0 = w.astype(BF16).astype(F32)
            r1 = w - w0
            w1 = r1.astype(BF16).astype(F32)
            w2 = r1 - w1
            lhs = jnp.where(sub == 0, w0, jnp.where(sub == 1, w1, jnp.where(sub == 2, w2, 0.0)))
            r = jnp.dot(lhs, _gathered_rows(buf), preferred_element_type=F32)
            o_ref[pl.ds(t0 + u, 1), :] = x_ref[pl.ds(t0 + u, 1), :] + (r[0:1] + r[1:2] + r[2:3])

    _token_pipeline(idx_ref, next_idx_ref, tab_ref, bufs, compute_group, split_regions=False)


def _idx_specs(n_steps):
    shape = (PEER_TOKENS, PEER_SLOTS)
    return [pl.BlockSpec(shape, lambda i: (i, 0), memory_space=pltpu.SMEM),
            pl.BlockSpec(shape, lambda i: (jnp.minimum(i + 1, n_steps - 1), 0), memory_space=pltpu.SMEM)]


def _table_spec():
    return pl.BlockSpec((N_EXPERTS * ROW_WORDS, 128), lambda i: (0, 0), pipeline_mode=pl.Buffered(1))


def _gather_buffers():
    return [pltpu.VMEM((G_ROWS, 128), jnp.uint32) for _ in range(2 * PEER_UNROLL)]


def _peer_u(rows, tab, xt, g_t):
    t = xt.shape[1]
    n_steps = t // PEER_TOKENS
    col = pl.BlockSpec((PEER_SLOTS, PEER_TOKENS), lambda i: (0, i))
    return pl.pallas_call(
        _peer_u_kernel,
        grid=(n_steps,),
        in_specs=_idx_specs(n_steps) + [_table_spec(),
                                        pl.BlockSpec((D_MODEL, PEER_TOKENS), lambda i: (0, i)), col],
        out_specs=col,
        out_shape=jax.ShapeDtypeStruct((PEER_SLOTS, t), F32),
        scratch_shapes=[pltpu.VMEM((D_MODEL, PEER_TOKENS), F32),
                        pltpu.VMEM((PEER_SLOTS, PEER_TOKENS), F32)] + _gather_buffers(),
        compiler_params=_cparams(1),
        name="peer_u",
    )(rows, rows, tab, xt, g_t)


def _peer_v(rows, tab, wgt_t, x):
    t = x.shape[0]
    n_steps = t // PEER_TOKENS
    row = pl.BlockSpec((PEER_TOKENS, D_MODEL), lambda i: (i, 0))
    col = pl.BlockSpec((PEER_SLOTS, PEER_TOKENS), lambda i: (0, i))
    return pl.pallas_call(
        _peer_v_kernel,
        grid=(n_steps,),
        in_specs=_idx_specs(n_steps) + [_table_spec(), col, row],
        out_specs=row,
        out_shape=jax.ShapeDtypeStruct((t, D_MODEL), F32),
        scratch_shapes=[pltpu.VMEM((PEER_TOKENS, PEER_SLOTS), F32)] + _gather_buffers(),
        compiler_params=_cparams(1),
        name="peer_v",
    )(rows, rows, tab, wgt_t, x)


def _final_norm_kernel(x_ref, g_ref, y_ref):
    x = x_ref[...]
    ms = jnp.mean(x * x, axis=-1, keepdims=True)
    y_ref[...] = x * lax.rsqrt(ms + RMS_EPS) * g_ref[...]


def _final_norm(x, g, first_row, n_rows):
    tm = TOKEN_TILE
    base = first_row // tm
    return pl.pallas_call(
        _final_norm_kernel,
        grid=(n_rows // tm,),
        in_specs=[pl.BlockSpec((tm, D_MODEL), lambda i: (base + i, 0)), _const_spec((1, D_MODEL))],
        out_specs=pl.BlockSpec((tm, D_MODEL), lambda i: (i, 0)),
        out_shape=jax.ShapeDtypeStruct((n_rows, D_MODEL), F32),
        compiler_params=_cparams(1),
        name="final_norm",
    )(x, g)


def kernel(x_prompt, x_sample, cache_k, cache_v, state_ssm_re, state_ssm_im, norm1_g, w_in, ssm_a_re, ssm_a_im, ssm_log_dt, ssm_b_re, ssm_b_im, ssm_c_re, ssm_c_im, ssm_d, ssm_glu_w, attn_rel_bias, attn_w_o, w_out, norm2_g, peer_w_q, peer_sub_keys, peer_u, peer_v, final_g):
    bp, lp, _ = x_prompt.shape
    bs, ls, _ = x_sample.shape
    depth = w_in.shape[0]
    tp = bp * lp
    kv_win = cache_k.shape[2]
    keep = min(LEFT_CHUNKS * CHUNK, lp)
    x = jnp.concatenate([x_prompt.reshape(tp, D_MODEL), x_sample.reshape(bs * ls, D_MODEL)], axis=0)
    t = x.shape[0]
    assert t % TOKEN_TILE == 0 and lp % SCAN_CHUNK == 0 and lp % CHUNK == 0 and kv_win == LEFT_CHUNKS * CHUNK

    zeros_state = jnp.zeros((bp, N_SSM), F32)
    outs = {n: [] for n in ("p_re", "p_im", "p_k", "p_v", "s_re", "s_im", "s_k", "s_v")}
    for l in range(depth):
        u, q, kv, kvb, gates = _inproj(x, norm1_g[l][None], w_in[l].astype(BF16))

        ssm_args = (ssm_a_re[l], ssm_a_im[l], ssm_log_dt[l], ssm_b_re[l], ssm_b_im[l], ssm_c_re[l], ssm_c_im[l])
        dskip = ssm_d[l][None]
        bcat, apr, api, ccat = _ssm_params(*ssm_args, SCAN_CHUNK)
        y, hr_p, hi_p = _ssm(u, None, bp, lp, 0, zeros_state, zeros_state,
                             bcat, apr, api, ccat, dskip, SCAN_CHUNK)
        y, hr_s, hi_s = _ssm(u, y, bs, ls, tp, state_ssm_re[l].reshape(bs, N_SSM),
                             state_ssm_im[l].reshape(bs, N_SSM), bcat, apr[:ls], api[:ls], ccat, dskip, ls)

        pad = ((0, 0), (LEFT_CHUNKS * CHUNK, 0), (0, 0))
        kb_p = jnp.pad(kvb[:tp, :D_ATTN].reshape(bp, lp, D_ATTN), pad)
        vb_p = jnp.pad(kvb[:tp, D_ATTN:].reshape(bp, lp, D_ATTN), pad)
        w_p = (LEFT_CHUNKS + 1) * CHUNK
        att = _attention(q, None, bp, lp, 0, kb_p, vb_p,
                         _bias_table(attn_rel_bias[l], CHUNK, w_p), CHUNK, w_p, True)
        kb_s = jnp.concatenate([cache_k[l].reshape(bs, kv_win, D_ATTN).astype(BF16),
                                kvb[tp:, :D_ATTN].reshape(bs, ls, D_ATTN)], axis=1)
        vb_s = jnp.concatenate([cache_v[l].reshape(bs, kv_win, D_ATTN).astype(BF16),
                                kvb[tp:, D_ATTN:].reshape(bs, ls, D_ATTN)], axis=1)
        w_s = kv_win + ls
        att = _attention(q, att, bs, ls, tp, kb_s, vb_s,
                         _bias_table(attn_rel_bias[l], ls, w_s), ls, w_s, False)

        x1, xn_t, qp = _mix(x, y, att, gates, ssm_glu_w[l].astype(BF16), attn_w_o[l].astype(BF16),
                          w_out[l].astype(BF16), norm2_g[l][None], peer_w_q[l].astype(BF16))
        keys = peer_sub_keys[l].reshape(2 * PEER_HEADS, N_KEYS, PEER_HALF).astype(BF16)
        rows, g_t = _route(qp, keys)
        wgt_t = _peer_u(rows, _pack_table(peer_u[l]), xn_t, g_t)
        x = _peer_v(rows, _pack_table(peer_v[l]), wgt_t, x1)

        kv_p = jnp.stack([kv[(i + 1) * lp - keep:(i + 1) * lp] for i in range(bp)])
        kv_p = kv_p.reshape(bp, keep, 2, N_HEADS, HEAD_DIM)
        kv_s = kv[tp:].reshape(bs, ls, 2, N_HEADS, HEAD_DIM)
        outs["p_re"].append(hr_p.reshape(bp, N_GROUPS, SSM_STATE))
        outs["p_im"].append(hi_p.reshape(bp, N_GROUPS, SSM_STATE))
        outs["p_k"].append(kv_p[:, :, 0])
        outs["p_v"].append(kv_p[:, :, 1])
        outs["s_re"].append(hr_s.reshape(bs, N_GROUPS, SSM_STATE))
        outs["s_im"].append(hi_s.reshape(bs, N_GROUPS, SSM_STATE))
        outs["s_k"].append(kv_s[:, :, 0])
        outs["s_v"].append(kv_s[:, :, 1])

    y_p = _final_norm(x, final_g[None], 0, tp)
    y_s = _final_norm(x, final_g[None], tp, bs * ls)
    st = {n: jnp.stack(v) for n, v in outs.items()}
    return (y_p.reshape(bp, lp, D_MODEL), y_s.reshape(bs, ls, D_MODEL),
            st["p_re"], st["p_im"], st["p_k"], st["p_v"], st["s_re"], st["s_im"], st["s_k"], st["s_v"])
```

```python
import functools
import math

import jax
import jax.numpy as jnp
import numpy as np
from jax import lax
from jax.experimental import pallas as pl
from jax.experimental.pallas import tpu as pltpu

F32 = jnp.float32
BF16 = jnp.bfloat16

D_MODEL = 1024
CHUNK = 64
D_SSM = 512
SSM_GROUP = 16
N_GROUPS = D_SSM // SSM_GROUP
SSM_STATE = 64
N_SSM = N_GROUPS * SSM_STATE
N_HEADS = 8
HEAD_DIM = 64
D_ATTN = N_HEADS * HEAD_DIM
LEFT_CHUNKS = 8
REL_CLIP = 128
ATTN_SCALE = HEAD_DIM ** -0.5
N_KEYS = 128
N_EXPERTS = N_KEYS * N_KEYS
PEER_HEADS = 8
PEER_DK = 256
PEER_HALF = PEER_DK // 2
PEER_TOPK = 16
PEER_SLOTS = PEER_HEADS * PEER_TOPK
RMS_EPS = 1e-6
D_IN = D_SSM + 3 * D_ATTN + 2 * D_MODEL

VMEM_LIMIT_BYTES = 56 * 1024 * 1024
TOKEN_TILE = 256
ROUTE_TILE = 128
ROUTE_HEADS_PER_STEP = 8
PEER_TOKENS = 128
PEER_UNROLL = 8
U_DOT_TOKENS = 4
SCAN_CHUNK = 256
SCAN_ROWS = 8
NEG_INF = float("-inf")

ROW_WORDS = 4
G_STRIDE = 136
G_ROWS = ROW_WORDS * G_STRIDE
IDX_QUAD = 4

PEER_CAND = [(i, j) for i in range(PEER_TOPK) for j in range(PEER_TOPK) if (i + 1) * (j + 1) <= PEER_TOPK]
N_CAND_ROWS = 56


def _cparams(n_axes):
    return pltpu.CompilerParams(dimension_semantics=("arbitrary",) * n_axes,
                                vmem_limit_bytes=VMEM_LIMIT_BYTES)


def _const_spec(shape):
    return pl.BlockSpec(shape, lambda *_: (0,) * len(shape))


def _inproj_kernel(x_ref, g_ref, w_ref, u_ref, q_ref, kv_ref, kvb_ref, gate_ref):
    x = x_ref[...]
    ms = jnp.mean(x * x, axis=-1, keepdims=True)
    xn = (x * lax.rsqrt(ms + RMS_EPS) * g_ref[...]).astype(BF16)

    def proj(lo, hi):
        return jnp.dot(xn, w_ref[:, lo:hi], preferred_element_type=F32)

    u_ref[...] = proj(0, D_SSM).astype(BF16)
    q_ref[...] = proj(D_SSM, D_SSM + D_ATTN).astype(BF16)
    for c in range(2):
        lo = D_SSM + D_ATTN + c * D_ATTN
        kv = proj(lo, lo + D_ATTN)
        kv_ref[:, c * D_ATTN:(c + 1) * D_ATTN] = kv
        kvb_ref[:, c * D_ATTN:(c + 1) * D_ATTN] = kv.astype(BF16)
    for c in range(4):
        lo = D_SSM + 3 * D_ATTN + c * 512
        gate_ref[:, c * 512:(c + 1) * 512] = jax.nn.sigmoid(proj(lo, lo + 512)).astype(BF16)


def _inproj(x, g, w_bf):
    t = x.shape[0]
    tm = TOKEN_TILE
    row = lambda n: pl.BlockSpec((tm, n), lambda i: (i, 0))
    return pl.pallas_call(
        _inproj_kernel,
        grid=(t // tm,),
        in_specs=[row(D_MODEL), _const_spec((1, D_MODEL)), _const_spec((D_MODEL, D_IN))],
        out_specs=[row(D_SSM), row(D_ATTN), row(2 * D_ATTN), row(2 * D_ATTN), row(2 * D_MODEL)],
        out_shape=[jax.ShapeDtypeStruct((t, D_SSM), BF16), jax.ShapeDtypeStruct((t, D_ATTN), BF16),
                   jax.ShapeDtypeStruct((t, 2 * D_ATTN), F32), jax.ShapeDtypeStruct((t, 2 * D_ATTN), BF16),
                   jax.ShapeDtypeStruct((t, 2 * D_MODEL), BF16)],
        compiler_params=_cparams(1),
        name="inproj",
    )(x, g, w_bf)


def _ssm_kernel(lc, u_ref, bcat_ref, apr_ref, api_ref, ccat_ref, d_ref, h0r_ref, h0i_ref, *rest):
    y_ref, hr_ref, hi_ref, re_ref, im_ref, cr_ref, ci_ref = rest[-7:]
    c = pl.program_id(1)

    @pl.when(c == 0)
    def _():
        cr_ref[...] = h0r_ref[0]
        ci_ref[...] = h0i_ref[0]

    u = u_ref[...]
    bu = jnp.dot(u, bcat_ref[...], preferred_element_type=F32)
    re_ref[...] = bu[:, :N_SSM]
    im_ref[...] = bu[:, N_SSM:]

    row = lax.broadcasted_iota(jnp.int32, (SCAN_ROWS, 128), 0)

    def lane_tile(j, _):
        sl = pl.ds(pl.multiple_of(j * 128, 128), 128)
        steps = []
        s = 1
        while s < SCAN_ROWS:
            steps.append((s, apr_ref[s - 1:s, sl], api_ref[s - 1:s, sl], row >= s))
            s *= 2
        pr = apr_ref[0:SCAN_ROWS, sl]
        pi = api_ref[0:SCAN_ROWS, sl]
        c_r = cr_ref[:, sl]
        c_i = ci_ref[:, sl]
        for blk in range(lc // SCAN_ROWS):
            rows = slice(blk * SCAN_ROWS, (blk + 1) * SCAN_ROWS)
            hr = re_ref[rows, sl]
            hi = im_ref[rows, sl]
            for s, ar, ai, keep in steps:
                sr = jnp.where(keep, pltpu.roll(hr, s, 0), 0.0)
                si = jnp.where(keep, pltpu.roll(hi, s, 0), 0.0)
                hr, hi = hr + (ar * sr - ai * si), hi + (ar * si + ai * sr)
            hr, hi = hr + (pr * c_r - pi * c_i), hi + (pr * c_i + pi * c_r)
            re_ref[rows, sl] = hr
            im_ref[rows, sl] = hi
            c_r = hr[SCAN_ROWS - 1:SCAN_ROWS]
            c_i = hi[SCAN_ROWS - 1:SCAN_ROWS]
        cr_ref[:, sl] = c_r
        ci_ref[:, sl] = c_i
        return 0

    lax.fori_loop(0, N_SSM // 128, lane_tile, 0)

    hr_ref[0] = cr_ref[...]
    hi_ref[0] = ci_ref[...]
    y = jnp.dot(re_ref[...].astype(BF16), ccat_ref[:N_SSM], preferred_element_type=F32)
    y = y + jnp.dot(im_ref[...].astype(BF16), ccat_ref[N_SSM:], preferred_element_type=F32)
    y_ref[...] = (y + d_ref[...] * u.astype(F32)).astype(BF16)


def _seq_rows_spec(rows, width, first_row, seq_len):
    per_seq = seq_len // rows
    base = first_row // rows
    return pl.BlockSpec((rows, width), lambda i, n: (base + i * per_seq + n, 0))


def _alias_prev(prev, n_inputs):
    if prev is None:
        return [], [], {}
    return [prev], [pl.BlockSpec(memory_space=pl.ANY)], {n_inputs: 0}


def _ssm(u, prev_y, b, l, first_row, h0r, h0i, bcat, apr, api, ccat, dskip, lc):
    st = jax.ShapeDtypeStruct((b, 1, N_SSM), F32)
    state_spec = pl.BlockSpec((1, 1, N_SSM), lambda i, c: (i, 0, 0))
    rows = _seq_rows_spec(lc, D_SSM, first_row, l)
    extra, extra_specs, aliases = _alias_prev(prev_y, 8)
    return pl.pallas_call(
        functools.partial(_ssm_kernel, lc),
        grid=(b, l // lc),
        in_specs=[rows, _const_spec((D_SSM, 2 * N_SSM)), _const_spec((lc, N_SSM)), _const_spec((lc, N_SSM)),
                  _const_spec((2 * N_SSM, D_SSM)), _const_spec((1, D_SSM)), state_spec, state_spec] + extra_specs,
        out_specs=[rows, state_spec, state_spec],
        out_shape=[jax.ShapeDtypeStruct(u.shape, BF16), st, st],
        scratch_shapes=[pltpu.VMEM((lc, N_SSM), F32), pltpu.VMEM((lc, N_SSM), F32),
                        pltpu.VMEM((1, N_SSM), F32), pltpu.VMEM((1, N_SSM), F32)],
        input_output_aliases=aliases,
        compiler_params=_cparams(2),
        name=f"ssm_scan_{lc}",
    )(u, bcat, apr, api, ccat, dskip, h0r.reshape(b, 1, N_SSM), h0i.reshape(b, 1, N_SSM), *extra)


def _ssm_params(a_re, a_im, log_dt, b_re, b_im, c_re, c_im, lc):
    a = lax.complex(a_re, a_im)
    adt = a * jnp.exp(log_dt)[:, None]
    a_bar = jnp.exp(adt)
    b_bar = ((a_bar - 1.0) / a)[..., None] * lax.complex(b_re, b_im)
    eye = jnp.eye(N_GROUPS, dtype=F32)
    bre = jnp.einsum("gpc,gh->gchp", jnp.real(b_bar), eye).reshape(D_SSM, N_SSM)
    bim = jnp.einsum("gpc,gh->gchp", jnp.imag(b_bar), eye).reshape(D_SSM, N_SSM)
    bcat = jnp.concatenate([bre, bim], axis=1).astype(BF16)
    cre = jnp.einsum("gcp,gh->gphc", c_re, eye).reshape(N_SSM, D_SSM)
    cim = jnp.einsum("gcp,gh->gphc", c_im, eye).reshape(N_SSM, D_SSM)
    ccat = jnp.concatenate([cre, -cim], axis=0).astype(BF16)
    steps = jnp.arange(1, lc + 1, dtype=F32)[:, None, None]
    apow = jnp.exp(adt[None] * steps).reshape(lc, N_SSM)
    return bcat, jnp.real(apow), jnp.imag(apow), ccat


def _attn_kernel(qc, w, masked, q_ref, k_ref, v_ref, bias_ref, *rest):
    o_ref = rest[-1]
    n = pl.program_id(1)
    second = lax.broadcasted_iota(jnp.int32, (qc, 2 * HEAD_DIM), 1) >= HEAD_DIM
    start = pl.multiple_of(n * qc, qc)
    scores = []
    for pr in range(N_HEADS // 2):
        lanes = slice(pr * 2 * HEAD_DIM, (pr + 1) * 2 * HEAD_DIM)
        q2 = q_ref[:, lanes]
        zero = jnp.zeros_like(q2)
        qs = jnp.concatenate([jnp.where(second, zero, q2), jnp.where(second, q2, zero)], axis=0)
        k2 = k_ref[0, pl.ds(start, w), lanes]
        scores.append(lax.dot_general(qs, k2, (((1,), (1,)), ((), ())), preferred_element_type=F32))
    s = jnp.concatenate(scores, axis=0)
    s = s * ATTN_SCALE + bias_ref[...]
    if masked:
        col = lax.broadcasted_iota(jnp.int32, (N_HEADS * qc, w), 1)
        s = jnp.where(col + n * qc >= LEFT_CHUNKS * CHUNK, s, jnp.finfo(F32).min)
    m = jnp.max(s, axis=-1, keepdims=True)
    p = jnp.exp(s - m)
    p = p / jnp.sum(p, axis=-1, keepdims=True)
    p = p.astype(BF16)
    for pr in range(N_HEADS // 2):
        lanes = slice(pr * 2 * HEAD_DIM, (pr + 1) * 2 * HEAD_DIM)
        v2 = v_ref[0, pl.ds(start, w), lanes]
        r = jnp.dot(p[2 * pr * qc:(2 * pr + 2) * qc], v2, preferred_element_type=F32)
        o_ref[:, lanes] = jnp.where(second, r[qc:], r[:qc]).astype(BF16)


def _attention(q, prev_att, b, l, first_row, k, v, bias, qc, w, masked):
    lk = k.shape[1]
    kv_spec = pl.BlockSpec((1, lk, D_ATTN), lambda i, n: (i, 0, 0))
    rows = _seq_rows_spec(qc, D_ATTN, first_row, l)
    extra, extra_specs, aliases = _alias_prev(prev_att, 4)
    return pl.pallas_call(
        functools.partial(_attn_kernel, qc, w, masked),
        grid=(b, l // qc),
        in_specs=[rows, kv_spec, kv_spec, _const_spec((N_HEADS * qc, w))] + extra_specs,
        out_specs=rows,
        out_shape=jax.ShapeDtypeStruct(q.shape, BF16),
        input_output_aliases=aliases,
        compiler_params=_cparams(2),
        name=f"band_attn_{qc}",
    )(q, k, v, bias, *extra)


def _bias_table(rel_bias, qc, w):
    lo = LEFT_CHUNKS * CHUNK + (qc - 1) - REL_CLIP
    hi = (w + qc - 1) - lo - (2 * REL_CLIP + 1)
    ext = jnp.pad(rel_bias.astype(F32), ((0, 0), (lo, max(hi, 0))), mode="edge")
    rows = [ext[:, qc - 1 - i:qc - 1 - i + w] for i in range(qc)]
    return jnp.stack(rows, axis=1).reshape(N_HEADS * qc, w)


def _mix_kernel(x_ref, y_ref, att_ref, gate_ref, glu_ref, wo_ref, wout_ref, g2_ref, wq_ref,
                x1_ref, xnt_ref, qp_ref):
    glu = jnp.dot(y_ref[...], glu_ref[...], preferred_element_type=F32)
    a = glu[:, :D_MODEL] * jax.nn.sigmoid(glu[:, D_MODEL:])
    b = jnp.dot(att_ref[...], wo_ref[...], preferred_element_type=F32)
    mixed = gate_ref[:, :D_MODEL].astype(F32) * a + gate_ref[:, D_MODEL:].astype(F32) * b
    x1 = x_ref[...] + jnp.dot(mixed.astype(BF16), wout_ref[...], preferred_element_type=F32)
    x1_ref[...] = x1
    ms = jnp.mean(x1 * x1, axis=-1, keepdims=True)
    xn = x1 * lax.rsqrt(ms + RMS_EPS) * g2_ref[...]
    xnt_ref[...] = xn.T.astype(BF16)
    qp_ref[...] = jnp.dot(xn.astype(BF16), wq_ref[...], preferred_element_type=F32).astype(BF16)


def _mix(x, y, att, gates, glu_w, w_o, w_out, g2, w_q):
    t = x.shape[0]
    tm = TOKEN_TILE
    row = lambda n: pl.BlockSpec((tm, n), lambda i: (i, 0))
    dq = PEER_HEADS * PEER_DK
    return pl.pallas_call(
        _mix_kernel,
        grid=(t // tm,),
        in_specs=[row(D_MODEL), row(D_SSM), row(D_ATTN), row(2 * D_MODEL),
                  _const_spec((D_SSM, 2 * D_MODEL)), _const_spec((D_ATTN, D_MODEL)),
                  _const_spec((D_MODEL, D_MODEL)), _const_spec((1, D_MODEL)), _const_spec((D_MODEL, dq))],
        out_specs=[row(D_MODEL), pl.BlockSpec((D_MODEL, tm), lambda i: (0, i)), row(dq)],
        out_shape=[jax.ShapeDtypeStruct((t, D_MODEL), F32), jax.ShapeDtypeStruct((D_MODEL, t), BF16),
                   jax.ShapeDtypeStruct((t, dq), BF16)],
        compiler_params=_cparams(1),
        name="mix",
    )(x, y, att, gates, glu_w, w_o, w_out, g2, w_q)


def _extract_max(s, row):
    m = jnp.max(s, axis=0, keepdims=True)
    pos = jnp.min(jnp.where(s == m, row, float(s.shape[0])), axis=0, keepdims=True)
    return m, pos, jnp.where(row == pos, NEG_INF, s)


def _route_kernel(qp_ref, keys_ref, rows_ref, g_ref, e_ref):
    tm = ROUTE_TILE
    row_k = lax.broadcasted_iota(jnp.int32, (N_KEYS, tm), 0).astype(F32)
    row_c = lax.broadcasted_iota(jnp.int32, (N_CAND_ROWS, tm), 0).astype(F32)
    sub_masks = [lax.broadcasted_iota(jnp.int32, (8, tm), 0) == r for r in range(8)]

    def stack_rows(rows, n_rows, fill):
        groups = []
        for first in range(0, n_rows, 8):
            acc = jnp.full((8, tm), fill, F32)
            for r, value in enumerate(rows[first:first + 8]):
                acc = jnp.where(sub_masks[r], value, acc)
            groups.append(acc)
        return jnp.concatenate(groups, axis=0)

    def head(h, _):
        vals, ids = [], []
        for z in range(2):
            hz = 2 * h + z
            q = qp_ref[:, pl.ds(pl.multiple_of(hz * PEER_HALF, PEER_HALF), PEER_HALF)]
            s = lax.dot_general(keys_ref[hz], q, (((1,), (1,)), ((), ())),
                                preferred_element_type=F32)
            v_z, i_z = [], []
            for _ in range(PEER_TOPK):
                m, pos, s = _extract_max(s, row_k)
                v_z.append(m)
                i_z.append(pos)
            vals.append(v_z)
            ids.append(i_z)
        cand = stack_rows([vals[0][i] + vals[1][j] for i, j in PEER_CAND], N_CAND_ROWS, NEG_INF)
        cand_e = stack_rows([ids[0][i] * float(N_KEYS) + ids[1][j] for i, j in PEER_CAND], N_CAND_ROWS, 0.0)
        best, best_e = [], []
        top = None
        for _ in range(PEER_TOPK):
            m, pos, cand = _extract_max(cand, row_c)
            top = m if top is None else top
            best.append(jnp.exp(m - top))
            best_e.append(jnp.max(jnp.where(row_c == pos, cand_e, -1.0), axis=0, keepdims=True))
        best = stack_rows(best, PEER_TOPK, 0.0)
        out_rows = pl.ds(pl.multiple_of(h * PEER_TOPK, PEER_TOPK), PEER_TOPK)
        e_ref[out_rows, :] = stack_rows(best_e, PEER_TOPK, 0.0)
        g_ref[out_rows, :] = best / jnp.sum(best, axis=0, keepdims=True)
        return 0

    def heads(i, _):
        for u in range(ROUTE_HEADS_PER_STEP):
            head(i * ROUTE_HEADS_PER_STEP + u, 0)
        return 0

    lax.fori_loop(0, PEER_HEADS // ROUTE_HEADS_PER_STEP, heads, 0)
    rows_ref[...] = e_ref[...].T.astype(jnp.int32) * ROW_WORDS


def _route(qp, keys_bf):
    t = qp.shape[0]
    tm = ROUTE_TILE
    out = pl.BlockSpec((PEER_SLOTS, tm), lambda i: (0, i))
    return pl.pallas_call(
        _route_kernel,
        grid=(t // tm,),
        in_specs=[pl.BlockSpec((tm, PEER_HEADS * PEER_DK), lambda i: (i, 0)),
                  _const_spec((2 * PEER_HEADS, N_KEYS, PEER_HALF))],
        out_specs=[pl.BlockSpec((tm, PEER_SLOTS), lambda i: (i, 0)), out],
        out_shape=[jax.ShapeDtypeStruct((t, PEER_SLOTS), jnp.int32),
                   jax.ShapeDtypeStruct((PEER_SLOTS, t), F32)],
        scratch_shapes=[pltpu.VMEM((PEER_SLOTS, tm), F32)],
        compiler_params=_cparams(1),
        name="peer_route",
    )(qp, keys_bf)


def _pack_table(tab):
    tb = tab.astype(BF16)
    hi = lax.bitcast_convert_type(tb[:, :512], jnp.uint16).astype(jnp.uint32)
    lo = lax.bitcast_convert_type(tb[:, 512:], jnp.uint16).astype(jnp.uint32)
    return ((hi << 16) | lo).reshape(N_EXPERTS * ROW_WORDS, 128)


def _gather_group(idx_ref, g, tab_ref, bufs):
    for u, buf in enumerate(bufs):
        for a in range(PEER_SLOTS // IDX_QUAD):
            quad = idx_ref.at[g * PEER_UNROLL + u, pl.ds(a * IDX_QUAD, IDX_QUAD)]
            for b in range(IDX_QUAD):
                k = a * IDX_QUAD + b
                row = pl.multiple_of(quad[b], ROW_WORDS)
                buf[pl.ds(k, ROW_WORDS, stride=G_STRIDE), :] = tab_ref[pl.ds(row, ROW_WORDS), :]


def _lane_tile(buf_ref, j):
    word = buf_ref[j * G_STRIDE:j * G_STRIDE + PEER_SLOTS, :]
    hi = pltpu.bitcast(word & jnp.uint32(0xFFFF0000), F32)
    lo = pltpu.bitcast(word << 16, F32)
    return hi, lo


def _token_pipeline(idx_ref, next_idx_ref, tab_ref, bufs, compute_group, split_regions):
    n_groups = PEER_TOKENS // PEER_UNROLL
    halves = (bufs[:PEER_UNROLL], bufs[PEER_UNROLL:])
    step = pl.program_id(0)

    @pl.when(step == 0)
    def _():
        _gather_group(idx_ref, 0, tab_ref, halves[0])

    def pair(g, half, src_ref, src_group):
        def run():
            _gather_group(src_ref, src_group, tab_ref, halves[1 - half])
            compute_group(g * PEER_UNROLL, halves[half])

        if split_regions:
            pl.when(step >= 0)(run)
        else:
            run()

    def body(i, _):
        for half in range(2):
            pair(2 * i + half, half, idx_ref, 2 * i + half + 1)
        return 0

    lax.fori_loop(0, n_groups // 2 - 1, body, 0)
    pair(n_groups - 2, 0, idx_ref, n_groups - 1)
    pair(n_groups - 1, 1, next_idx_ref, 0)


def _gathered_rows(buf):
    tiles = [_lane_tile(buf, j) for j in range(ROW_WORDS)]
    return jnp.concatenate([tl[0] for tl in tiles] + [tl[1] for tl in tiles], axis=1)


def _peer_u_kernel(idx_ref, next_idx_ref, tab_ref, xt_ref, g_ref, w_ref, xw_ref, act_ref, *bufs):
    lane = lax.broadcasted_iota(jnp.int32, (PEER_SLOTS, PEER_TOKENS), 1)
    xw_ref[...] = xt_ref[...].astype(F32)

    def compute_group(t0, bufs):
        for first in range(0, PEER_UNROLL, U_DOT_TOKENS):
            part = bufs[first:first + U_DOT_TOKENS]
            lhs = jnp.concatenate([_gathered_rows(b) for b in part], axis=0)
            r = jnp.dot(lhs, xw_ref[...], preferred_element_type=F32)
            for u in range(U_DOT_TOKENS):
                pltpu.store(act_ref, r[u * PEER_SLOTS:(u + 1) * PEER_SLOTS], mask=lane == t0 + first + u)

    _token_pipeline(idx_ref, next_idx_ref, tab_ref, bufs, compute_group, split_regions=True)
    act = act_ref[...]
    gelu = 0.5 * act * (1.0 + lax.erf(act * (1.0 / math.sqrt(2.0))))
    w_ref[...] = g_ref[...] * gelu


def _peer_v_kernel(idx_ref, next_idx_ref, tab_ref, wgt_ref, x_ref, o_ref, wt_ref, *bufs):
    sub = lax.broadcasted_iota(jnp.int32, (8, PEER_SLOTS), 0)
    wt_ref[...] = wgt_ref[...].T

    def compute_group(t0, bufs):
        for u, buf in enumerate(bufs):
            w = wt_ref[pl.ds(t0 + u, 1), :]
            w0 = w.astype(BF16).astype(F32)
            r1 = w - w0
            w1 = r1.astype(BF16).astype(F32)
            w2 = r1 - w1
            lhs = jnp.where(sub == 0, w0, jnp.where(sub == 1, w1, jnp.where(sub == 2, w2, 0.0)))
            r = jnp.dot(lhs, _gathered_rows(buf), preferred_element_type=F32)
            o_ref[pl.ds(t0 + u, 1), :] = x_ref[pl.ds(t0 + u, 1), :] + (r[0:1] + r[1:2] + r[2:3])

    _token_pipeline(idx_ref, next_idx_ref, tab_ref, bufs, compute_group, split_regions=False)


def _idx_specs(n_steps):
    shape = (PEER_TOKENS, PEER_SLOTS)
    return [pl.BlockSpec(shape, lambda i: (i, 0), memory_space=pltpu.SMEM),
            pl.BlockSpec(shape, lambda i: (jnp.minimum(i + 1, n_steps - 1), 0), memory_space=pltpu.SMEM)]


def _table_spec():
    return pl.BlockSpec((N_EXPERTS * ROW_WORDS, 128), lambda i: (0, 0), pipeline_mode=pl.Buffered(1))


def _gather_buffers():
    return [pltpu.VMEM((G_ROWS, 128), jnp.uint32) for _ in range(2 * PEER_UNROLL)]


def _peer_u(rows, tab, xt, g_t):
    t = xt.shape[1]
    n_steps = t // PEER_TOKENS
    col = pl.BlockSpec((PEER_SLOTS, PEER_TOKENS), lambda i: (0, i))
    return pl.pallas_call(
        _peer_u_kernel,
        grid=(n_steps,),
        in_specs=_idx_specs(n_steps) + [_table_spec(),
                                        pl.BlockSpec((D_MODEL, PEER_TOKENS), lambda i: (0, i)), col],
        out_specs=col,
        out_shape=jax.ShapeDtypeStruct((PEER_SLOTS, t), F32),
        scratch_shapes=[pltpu.VMEM((D_MODEL, PEER_TOKENS), F32),
                        pltpu.VMEM((PEER_SLOTS, PEER_TOKENS), F32)] + _gather_buffers(),
        compiler_params=_cparams(1),
        name="peer_u",
    )(rows, rows, tab, xt, g_t)


def _peer_v(rows, tab, wgt_t, x):
    t = x.shape[0]
    n_steps = t // PEER_TOKENS
    row = pl.BlockSpec((PEER_TOKENS, D_MODEL), lambda i: (i, 0))
    col = pl.BlockSpec((PEER_SLOTS, PEER_TOKENS), lambda i: (0, i))
    return pl.pallas_call(
        _peer_v_kernel,
        grid=(n_steps,),
        in_specs=_idx_specs(n_steps) + [_table_spec(), col, row],
        out_specs=row,
        out_shape=jax.ShapeDtypeStruct((t, D_MODEL), F32),
        scratch_shapes=[pltpu.VMEM((PEER_TOKENS, PEER_SLOTS), F32)] + _gather_buffers(),
        compiler_params=_cparams(1),
        name="peer_v",
    )(rows, rows, tab, wgt_t, x)


def _final_norm_kernel(x_ref, g_ref, y_ref):
    x = x_ref[...]
    ms = jnp.mean(x * x, axis=-1, keepdims=True)
    y_ref[...] = x * lax.rsqrt(ms + RMS_EPS) * g_ref[...]


def _final_norm(x, g, first_row, n_rows):
    tm = TOKEN_TILE
    base = first_row // tm
    return pl.pallas_call(
        _final_norm_kernel,
        grid=(n_rows // tm,),
        in_specs=[pl.BlockSpec((tm, D_MODEL), lambda i: (base + i, 0)), _const_spec((1, D_MODEL))],
        out_specs=pl.BlockSpec((tm, D_MODEL), lambda i: (i, 0)),
        out_shape=jax.ShapeDtypeStruct((n_rows, D_MODEL), F32),
        compiler_params=_cparams(1),
        name="final_norm",
    )(x, g)


def kernel(x_prompt, x_sample, cache_k, cache_v, state_ssm_re, state_ssm_im, norm1_g, w_in, ssm_a_re, ssm_a_im, ssm_log_dt, ssm_b_re, ssm_b_im, ssm_c_re, ssm_c_im, ssm_d, ssm_glu_w, attn_rel_bias, attn_w_o, w_out, norm2_g, peer_w_q, peer_sub_keys, peer_u, peer_v, final_g):
    bp, lp, _ = x_prompt.shape
    bs, ls, _ = x_sample.shape
    depth = w_in.shape[0]
    tp = bp * lp
    kv_win = cache_k.shape[2]
    keep = min(LEFT_CHUNKS * CHUNK, lp)
    x = jnp.concatenate([x_prompt.reshape(tp, D_MODEL), x_sample.reshape(bs * ls, D_MODEL)], axis=0)
    t = x.shape[0]
    assert t % TOKEN_TILE == 0 and lp % SCAN_CHUNK == 0 and lp % CHUNK == 0 and kv_win == LEFT_CHUNKS * CHUNK

    zeros_state = jnp.zeros((bp, N_SSM), F32)
    outs = {n: [] for n in ("p_re", "p_im", "p_k", "p_v", "s_re", "s_im", "s_k", "s_v")}
    for l in range(depth):
        u, q, kv, kvb, gates = _inproj(x, norm1_g[l][None], w_in[l].astype(BF16))

        ssm_args = (ssm_a_re[l], ssm_a_im[l], ssm_log_dt[l], ssm_b_re[l], ssm_b_im[l], ssm_c_re[l], ssm_c_im[l])
        dskip = ssm_d[l][None]
        bcat, apr, api, ccat = _ssm_params(*ssm_args, SCAN_CHUNK)
        y, hr_p, hi_p = _ssm(u, None, bp, lp, 0, zeros_state, zeros_state,
                             bcat, apr, api, ccat, dskip, SCAN_CHUNK)
        y, hr_s, hi_s = _ssm(u, y, bs, ls, tp, state_ssm_re[l].reshape(bs, N_SSM),
                             state_ssm_im[l].reshape(bs, N_SSM), bcat, apr[:ls], api[:ls], ccat, dskip, ls)

        pad = ((0, 0), (LEFT_CHUNKS * CHUNK, 0), (0, 0))
        kb_p = jnp.pad(kvb[:tp, :D_ATTN].reshape(bp, lp, D_ATTN), pad)
        vb_p = jnp.pad(kvb[:tp, D_ATTN:].reshape(bp, lp, D_ATTN), pad)
        w_p = (LEFT_CHUNKS + 1) * CHUNK
        att = _attention(q, None, bp, lp, 0, kb_p, vb_p,
                         _bias_table(attn_rel_bias[l], CHUNK, w_p), CHUNK, w_p, True)
        kb_s = jnp.concatenate([cache_k[l].reshape(bs, kv_win, D_ATTN).astype(BF16),
                                kvb[tp:, :D_ATTN].reshape(bs, ls, D_ATTN)], axis=1)
        vb_s = jnp.concatenate([cache_v[l].reshape(bs, kv_win, D_ATTN).astype(BF16),
                                kvb[tp:, D_ATTN:].reshape(bs, ls, D_ATTN)], axis=1)
        w_s = kv_win + ls
        att = _attention(q, att, bs, ls, tp, kb_s, vb_s,
                         _bias_table(attn_rel_bias[l], ls, w_s), ls, w_s, False)

        x1, xn_t, qp = _mix(x, y, att, gates, ssm_glu_w[l].astype(BF16), attn_w_o[l].astype(BF16),
                          w_out[l].astype(BF16), norm2_g[l][None], peer_w_q[l].astype(BF16))
        keys = peer_sub_keys[l].reshape(2 * PEER_HEADS, N_KEYS, PEER_HALF).astype(BF16)
        rows, g_t = _route(qp, keys)
        wgt_t = _peer_u(rows, _pack_table(peer_u[l]), xn_t, g_t)
        x = _peer_v(rows, _pack_table(peer_v[l]), wgt_t, x1)

        kv_p = jnp.stack([kv[(i + 1) * lp - keep:(i + 1) * lp] for i in range(bp)])
        kv_p = kv_p.reshape(bp, keep, 2, N_HEADS, HEAD_DIM)
        kv_s = kv[tp:].reshape(bs, ls, 2, N_HEADS, HEAD_DIM)
        outs["p_re"].append(hr_p.reshape(bp, N_GROUPS, SSM_STATE))
        outs["p_im"].append(hi_p.reshape(bp, N_GROUPS, SSM_STATE))
        outs["p_k"].append(kv_p[:, :, 0])
        outs["p_v"].append(kv_p[:, :, 1])
        outs["s_re"].append(hr_s.reshape(bs, N_GROUPS, SSM_STATE))
        outs["s_im"].append(hi_s.reshape(bs, N_GROUPS, SSM_STATE))
        outs["s_k"].append(kv_s[:, :, 0])
        outs["s_v"].append(kv_s[:, :, 1])

    y_p = _final_norm(x, final_g[None], 0, tp)
    y_s = _final_norm(x, final_g[None], tp, bs * ls)
    st = {n: jnp.stack(v) for n, v in outs.items()}
    return (y_p.reshape(bp, lp, D_MODEL), y_s.reshape(bs, ls, D_MODEL),
            st["p_re"], st["p_im"], st["p_k"], st["p_v"], st["s_re"], st["s_im"], st["s_k"], st["s_v"])
```

```python
import functools
import math

import jax
import jax.numpy as jnp
import numpy as np
from jax import lax
from jax.experimental import pallas as pl
from jax.experimental.pallas import tpu as pltpu

F32 = jnp.float32
BF16 = jnp.bfloat16

D_MODEL = 1024
CHUNK = 64
D_SSM = 512
SSM_GROUP = 16
N_GROUPS = D_SSM // SSM_GROUP
SSM_STATE = 64
N_SSM = N_GROUPS * SSM_STATE
N_HEADS = 8
HEAD_DIM = 64
D_ATTN = N_HEADS * HEAD_DIM
LEFT_CHUNKS = 8
REL_CLIP = 128
ATTN_SCALE = HEAD_DIM ** -0.5
N_KEYS = 128
N_EXPERTS = N_KEYS * N_KEYS
PEER_HEADS = 8
PEER_DK = 256
PEER_HALF = PEER_DK // 2
PEER_TOPK = 16
PEER_SLOTS = PEER_HEADS * PEER_TOPK
RMS_EPS = 1e-6
D_IN = D_SSM + 3 * D_ATTN + 2 * D_MODEL

VMEM_LIMIT_BYTES = 56 * 1024 * 1024
TOKEN_TILE = 256
ROUTE_TILE = 128
ROUTE_HEADS_PER_STEP = 8
PEER_TOKENS = 128
PEER_UNROLL = 8
U_DOT_TOKENS = 4
SCAN_CHUNK = 256
SCAN_ROWS = 8
NEG_INF = float("-inf")

ROW_WORDS = 4
G_STRIDE = 136
G_ROWS = ROW_WORDS * G_STRIDE
IDX_QUAD = 8

PEER_CAND = [(i, j) for i in range(PEER_TOPK) for j in range(PEER_TOPK) if (i + 1) * (j + 1) <= PEER_TOPK]
N_CAND_ROWS = 56


def _cparams(n_axes):
    return pltpu.CompilerParams(dimension_semantics=("arbitrary",) * n_axes,
                                vmem_limit_bytes=VMEM_LIMIT_BYTES)


def _const_spec(shape):
    return pl.BlockSpec(shape, lambda *_: (0,) * len(shape))


def _inproj_kernel(x_ref, g_ref, w_ref, u_ref, q_ref, kv_ref, kvb_ref, gate_ref):
    x = x_ref[...]
    ms = jnp.mean(x * x, axis=-1, keepdims=True)
    xn = (x * lax.rsqrt(ms + RMS_EPS) * g_ref[...]).astype(BF16)

    def proj(lo, hi):
        return jnp.dot(xn, w_ref[:, lo:hi], preferred_element_type=F32)

    u_ref[...] = proj(0, D_SSM).astype(BF16)
    q_ref[...] = proj(D_SSM, D_SSM + D_ATTN).astype(BF16)
    for c in range(2):
        lo = D_SSM + D_ATTN + c * D_ATTN
        kv = proj(lo, lo + D_ATTN)
        kv_ref[:, c * D_ATTN:(c + 1) * D_ATTN] = kv
        kvb_ref[:, c * D_ATTN:(c + 1) * D_ATTN] = kv.astype(BF16)
    for c in range(4):
        lo = D_SSM + 3 * D_ATTN + c * 512
        gate_ref[:, c * 512:(c + 1) * 512] = jax.nn.sigmoid(proj(lo, lo + 512)).astype(BF16)


def _inproj(x, g, w_bf):
    t = x.shape[0]
    tm = TOKEN_TILE
    row = lambda n: pl.BlockSpec((tm, n), lambda i: (i, 0))
    return pl.pallas_call(
        _inproj_kernel,
        grid=(t // tm,),
        in_specs=[row(D_MODEL), _const_spec((1, D_MODEL)), _const_spec((D_MODEL, D_IN))],
        out_specs=[row(D_SSM), row(D_ATTN), row(2 * D_ATTN), row(2 * D_ATTN), row(2 * D_MODEL)],
        out_shape=[jax.ShapeDtypeStruct((t, D_SSM), BF16), jax.ShapeDtypeStruct((t, D_ATTN), BF16),
                   jax.ShapeDtypeStruct((t, 2 * D_ATTN), F32), jax.ShapeDtypeStruct((t, 2 * D_ATTN), BF16),
                   jax.ShapeDtypeStruct((t, 2 * D_MODEL), BF16)],
        compiler_params=_cparams(1),
        name="inproj",
    )(x, g, w_bf)


def _ssm_kernel(lc, u_ref, bcat_ref, apr_ref, api_ref, ccat_ref, d_ref, h0r_ref, h0i_ref, *rest):
    y_ref, hr_ref, hi_ref, re_ref, im_ref, cr_ref, ci_ref = rest[-7:]
    c = pl.program_id(1)

    @pl.when(c == 0)
    def _():
        cr_ref[...] = h0r_ref[0]
        ci_ref[...] = h0i_ref[0]

    u = u_ref[...]
    hc, hs = D_SSM // 2, N_SSM // 2
    for half in range(2):
        uh = u[:, half * hc:(half + 1) * hc]
        st = slice(half * hs, (half + 1) * hs)
        re_ref[:, st] = jnp.dot(uh, bcat_ref[half * hc:(half + 1) * hc, st], preferred_element_type=F32)
        im_ref[:, st] = jnp.dot(uh, bcat_ref[half * hc:(half + 1) * hc, N_SSM + half * hs:N_SSM + (half + 1) * hs],
                                preferred_element_type=F32)

    row = lax.broadcasted_iota(jnp.int32, (SCAN_ROWS, 128), 0)

    def lane_tile(j, _):
        sl = pl.ds(pl.multiple_of(j * 128, 128), 128)
        steps = []
        s = 1
        while s < SCAN_ROWS:
            steps.append((s, apr_ref[s - 1:s, sl], api_ref[s - 1:s, sl], row >= s))
            s *= 2
        pr = apr_ref[0:SCAN_ROWS, sl]
        pi = api_ref[0:SCAN_ROWS, sl]
        c_r = cr_ref[:, sl]
        c_i = ci_ref[:, sl]
        for blk in range(lc // SCAN_ROWS):
            rows = slice(blk * SCAN_ROWS, (blk + 1) * SCAN_ROWS)
            hr = re_ref[rows, sl]
            hi = im_ref[rows, sl]
            for s, ar, ai, keep in steps:
                sr = jnp.where(keep, pltpu.roll(hr, s, 0), 0.0)
                si = jnp.where(keep, pltpu.roll(hi, s, 0), 0.0)
                hr, hi = hr + (ar * sr - ai * si), hi + (ar * si + ai * sr)
            hr, hi = hr + (pr * c_r - pi * c_i), hi + (pr * c_i + pi * c_r)
            re_ref[rows, sl] = hr
            im_ref[rows, sl] = hi
            c_r = hr[SCAN_ROWS - 1:SCAN_ROWS]
            c_i = hi[SCAN_ROWS - 1:SCAN_ROWS]
        cr_ref[:, sl] = c_r
        ci_ref[:, sl] = c_i
        return 0

    lax.fori_loop(0, N_SSM // 128, lane_tile, 0)

    hr_ref[0] = cr_ref[...]
    hi_ref[0] = ci_ref[...]
    for half in range(2):
        st = slice(half * hs, (half + 1) * hs)
        ch = slice(half * hc, (half + 1) * hc)
        y = jnp.dot(re_ref[:, st].astype(BF16), ccat_ref[half * hs:(half + 1) * hs, ch],
                    preferred_element_type=F32)
        y = y + jnp.dot(im_ref[:, st].astype(BF16), ccat_ref[N_SSM + half * hs:N_SSM + (half + 1) * hs, ch],
                        preferred_element_type=F32)
        y_ref[:, ch] = (y + d_ref[:, ch] * u[:, ch].astype(F32)).astype(BF16)


def _seq_rows_spec(rows, width, first_row, seq_len):
    per_seq = seq_len // rows
    base = first_row // rows
    return pl.BlockSpec((rows, width), lambda i, n: (base + i * per_seq + n, 0))


def _alias_prev(prev, n_inputs):
    if prev is None:
        return [], [], {}
    return [prev], [pl.BlockSpec(memory_space=pl.ANY)], {n_inputs: 0}


def _ssm(u, prev_y, b, l, first_row, h0r, h0i, bcat, apr, api, ccat, dskip, lc):
    st = jax.ShapeDtypeStruct((b, 1, N_SSM), F32)
    state_spec = pl.BlockSpec((1, 1, N_SSM), lambda i, c: (i, 0, 0))
    rows = _seq_rows_spec(lc, D_SSM, first_row, l)
    extra, extra_specs, aliases = _alias_prev(prev_y, 8)
    return pl.pallas_call(
        functools.partial(_ssm_kernel, lc),
        grid=(b, l // lc),
        in_specs=[rows, _const_spec((D_SSM, 2 * N_SSM)), _const_spec((lc, N_SSM)), _const_spec((lc, N_SSM)),
                  _const_spec((2 * N_SSM, D_SSM)), _const_spec((1, D_SSM)), state_spec, state_spec] + extra_specs,
        out_specs=[rows, state_spec, state_spec],
        out_shape=[jax.ShapeDtypeStruct(u.shape, BF16), st, st],
        scratch_shapes=[pltpu.VMEM((lc, N_SSM), F32), pltpu.VMEM((lc, N_SSM), F32),
                        pltpu.VMEM((1, N_SSM), F32), pltpu.VMEM((1, N_SSM), F32)],
        input_output_aliases=aliases,
        compiler_params=_cparams(2),
        name=f"ssm_scan_{lc}",
    )(u, bcat, apr, api, ccat, dskip, h0r.reshape(b, 1, N_SSM), h0i.reshape(b, 1, N_SSM), *extra)


def _ssm_params(a_re, a_im, log_dt, b_re, b_im, c_re, c_im, lc):
    a = lax.complex(a_re, a_im)
    adt = a * jnp.exp(log_dt)[:, None]
    a_bar = jnp.exp(adt)
    b_bar = ((a_bar - 1.0) / a)[..., None] * lax.complex(b_re, b_im)
    eye = jnp.eye(N_GROUPS, dtype=F32)
    bre = jnp.einsum("gpc,gh->gchp", jnp.real(b_bar), eye).reshape(D_SSM, N_SSM)
    bim = jnp.einsum("gpc,gh->gchp", jnp.imag(b_bar), eye).reshape(D_SSM, N_SSM)
    bcat = jnp.concatenate([bre, bim], axis=1).astype(BF16)
    cre = jnp.einsum("gcp,gh->gphc", c_re, eye).reshape(N_SSM, D_SSM)
    cim = jnp.einsum("gcp,gh->gphc", c_im, eye).reshape(N_SSM, D_SSM)
    ccat = jnp.concatenate([cre, -cim], axis=0).astype(BF16)
    steps = jnp.arange(1, lc + 1, dtype=F32)[:, None, None]
    apow = jnp.exp(adt[None] * steps).reshape(lc, N_SSM)
    return bcat, jnp.real(apow), jnp.imag(apow), ccat


def _attn_kernel(qc, w, masked, q_ref, k_ref, v_ref, bias_ref, *rest):
    o_ref = rest[-1]
    n = pl.program_id(1)
    second = lax.broadcasted_iota(jnp.int32, (qc, 2 * HEAD_DIM), 1) >= HEAD_DIM
    start = pl.multiple_of(n * qc, qc)
    scores = []
    for pr in range(N_HEADS // 2):
        lanes = slice(pr * 2 * HEAD_DIM, (pr + 1) * 2 * HEAD_DIM)
        q2 = q_ref[:, lanes]
        zero = jnp.zeros_like(q2)
        qs = jnp.concatenate([jnp.where(second, zero, q2), jnp.where(second, q2, zero)], axis=0)
        k2 = k_ref[0, pl.ds(start, w), lanes]
        scores.append(lax.dot_general(qs, k2, (((1,), (1,)), ((), ())), preferred_element_type=F32))
    s = jnp.concatenate(scores, axis=0)
    s = s * ATTN_SCALE + bias_ref[...]
    if masked:
        col = lax.broadcasted_iota(jnp.int32, (N_HEADS * qc, w), 1)
        s = jnp.where(col + n * qc >= LEFT_CHUNKS * CHUNK, s, jnp.finfo(F32).min)
    m = jnp.max(s, axis=-1, keepdims=True)
    p = jnp.exp(s - m)
    p = p / jnp.sum(p, axis=-1, keepdims=True)
    p = p.astype(BF16)
    for pr in range(N_HEADS // 2):
        lanes = slice(pr * 2 * HEAD_DIM, (pr + 1) * 2 * HEAD_DIM)
        v2 = v_ref[0, pl.ds(start, w), lanes]
        r = jnp.dot(p[2 * pr * qc:(2 * pr + 2) * qc], v2, preferred_element_type=F32)
        o_ref[:, lanes] = jnp.where(second, r[qc:], r[:qc]).astype(BF16)


def _attention(q, prev_att, b, l, first_row, k, v, bias, qc, w, masked):
    lk = k.shape[1]
    kv_spec = pl.BlockSpec((1, lk, D_ATTN), lambda i, n: (i, 0, 0))
    rows = _seq_rows_spec(qc, D_ATTN, first_row, l)
    extra, extra_specs, aliases = _alias_prev(prev_att, 4)
    return pl.pallas_call(
        functools.partial(_attn_kernel, qc, w, masked),
        grid=(b, l // qc),
        in_specs=[rows, kv_spec, kv_spec, _const_spec((N_HEADS * qc, w))] + extra_specs,
        out_specs=rows,
        out_shape=jax.ShapeDtypeStruct(q.shape, BF16),
        input_output_aliases=aliases,
        compiler_params=_cparams(2),
        name=f"band_attn_{qc}",
    )(q, k, v, bias, *extra)


def _bias_table(rel_bias, qc, w):
    lo = LEFT_CHUNKS * CHUNK + (qc - 1) - REL_CLIP
    hi = (w + qc - 1) - lo - (2 * REL_CLIP + 1)
    ext = jnp.pad(rel_bias.astype(F32), ((0, 0), (lo, max(hi, 0))), mode="edge")
    rows = [ext[:, qc - 1 - i:qc - 1 - i + w] for i in range(qc)]
    return jnp.stack(rows, axis=1).reshape(N_HEADS * qc, w)


def _mix_kernel(x_ref, y_ref, att_ref, gate_ref, glu_ref, wo_ref, wout_ref, g2_ref, wq_ref,
                x1_ref, xnt_ref, qp_ref):
    glu = jnp.dot(y_ref[...], glu_ref[...], preferred_element_type=F32)
    a = glu[:, :D_MODEL] * jax.nn.sigmoid(glu[:, D_MODEL:])
    b = jnp.dot(att_ref[...], wo_ref[...], preferred_element_type=F32)
    mixed = gate_ref[:, :D_MODEL].astype(F32) * a + gate_ref[:, D_MODEL:].astype(F32) * b
    x1 = x_ref[...] + jnp.dot(mixed.astype(BF16), wout_ref[...], preferred_element_type=F32)
    x1_ref[...] = x1
    ms = jnp.mean(x1 * x1, axis=-1, keepdims=True)
    xn = x1 * lax.rsqrt(ms + RMS_EPS) * g2_ref[...]
    xnt_ref[...] = xn.T.astype(BF16)
    qp_ref[...] = jnp.dot(xn.astype(BF16), wq_ref[...], preferred_element_type=F32).astype(BF16)


def _mix(x, y, att, gates, glu_w, w_o, w_out, g2, w_q):
    t = x.shape[0]
    tm = TOKEN_TILE
    row = lambda n: pl.BlockSpec((tm, n), lambda i: (i, 0))
    dq = PEER_HEADS * PEER_DK
    return pl.pallas_call(
        _mix_kernel,
        grid=(t // tm,),
        in_specs=[row(D_MODEL), row(D_SSM), row(D_ATTN), row(2 * D_MODEL),
                  _const_spec((D_SSM, 2 * D_MODEL)), _const_spec((D_ATTN, D_MODEL)),
                  _const_spec((D_MODEL, D_MODEL)), _const_spec((1, D_MODEL)), _const_spec((D_MODEL, dq))],
        out_specs=[row(D_MODEL), pl.BlockSpec((D_MODEL, tm), lambda i: (0, i)), row(dq)],
        out_shape=[jax.ShapeDtypeStruct((t, D_MODEL), F32), jax.ShapeDtypeStruct((D_MODEL, t), BF16),
                   jax.ShapeDtypeStruct((t, dq), BF16)],
        compiler_params=_cparams(1),
        name="mix",
    )(x, y, att, gates, glu_w, w_o, w_out, g2, w_q)


def _extract_max(s, row):
    m = jnp.max(s, axis=0, keepdims=True)
    pos = jnp.min(jnp.where(s == m, row, float(s.shape[0])), axis=0, keepdims=True)
    return m, pos, jnp.where(row == pos, NEG_INF, s)


def _top_keys(s, key_ids):
    n = N_KEYS // 4
    v = [s[i * n:(i + 1) * n] for i in range(4)]
    ids = list(key_ids)

    def order(a, b, ids_ordered):
        first = v[a] >= v[b] if ids_ordered else (v[a] > v[b]) | ((v[a] == v[b]) & (ids[a] < ids[b]))
        v[a], v[b] = jnp.where(first, v[a], v[b]), jnp.where(first, v[b], v[a])
        ids[a], ids[b] = jnp.where(first, ids[a], ids[b]), jnp.where(first, ids[b], ids[a])

    order(0, 1, True)
    order(2, 3, True)
    order(0, 2, True)
    order(1, 3, True)
    order(1, 2, False)
    top_v, top_i = [], []
    for _ in range(PEER_TOPK):
        m = jnp.max(v[0], axis=0, keepdims=True)
        key = jnp.min(jnp.where(v[0] == m, ids[0], float(N_KEYS)), axis=0, keepdims=True)
        won = ids[0] == key
        for level in range(3):
            v[level] = jnp.where(won, v[level + 1], v[level])
            ids[level] = jnp.where(won, ids[level + 1], ids[level])
        v[3] = jnp.where(won, NEG_INF, v[3])
        top_v.append(m)
        top_i.append(key)
    return top_v, top_i


def _route_kernel(qp_ref, keys_ref, rows_ref, g_ref, e_ref):
    tm = ROUTE_TILE
    quarter = lax.broadcasted_iota(jnp.int32, (N_KEYS // 4, tm), 0).astype(F32)
    key_ids = [quarter + float(i * (N_KEYS // 4)) for i in range(4)]
    row_c =lax.broadcasted_iota(jnp.int32, (N_CAND_ROWS, tm), 0).astype(F32)
    sub_masks = [lax.broadcasted_iota(jnp.int32, (8, tm), 0) == r for r in range(8)]

    def stack_rows(rows, n_rows, fill):
        groups = []
        for first in range(0, n_rows, 8):
            acc = jnp.full((8, tm), fill, F32)
            for r, value in enumerate(rows[first:first + 8]):
                acc = jnp.where(sub_masks[r], value, acc)
            groups.append(acc)
        return jnp.concatenate(groups, axis=0)

    def head(h, _):
        vals, ids = [], []
        for z in range(2):
            hz = 2 * h + z
            q = qp_ref[:, pl.ds(pl.multiple_of(hz * PEER_HALF, PEER_HALF), PEER_HALF)]
            s = lax.dot_general(keys_ref[hz], q, (((1,), (1,)), ((), ())),
                                preferred_element_type=F32)
            v_z, i_z = _top_keys(s, key_ids)
            vals.append(v_z)
            ids.append(i_z)
        cand = stack_rows([vals[0][i] + vals[1][j] for i, j in PEER_CAND], N_CAND_ROWS, NEG_INF)
        cand_e = stack_rows([ids[0][i] * float(N_KEYS) + ids[1][j] for i, j in PEER_CAND], N_CAND_ROWS, 0.0)
        best, best_e = [], []
        top = None
        for _ in range(PEER_TOPK):
            m, pos, cand = _extract_max(cand, row_c)
            top = m if top is None else top
            best.append(jnp.exp(m - top))
            best_e.append(jnp.max(jnp.where(row_c == pos, cand_e, -1.0), axis=0, keepdims=True))
        best = stack_rows(best, PEER_TOPK, 0.0)
        out_rows = pl.ds(pl.multiple_of(h * PEER_TOPK, PEER_TOPK), PEER_TOPK)
        e_ref[out_rows, :] = stack_rows(best_e, PEER_TOPK, 0.0)
        g_ref[out_rows, :] = best / jnp.sum(best, axis=0, keepdims=True)
        return 0

    def heads(i, _):
        for u in range(ROUTE_HEADS_PER_STEP):
            head(i * ROUTE_HEADS_PER_STEP + u, 0)
        return 0

    lax.fori_loop(0, PEER_HEADS // ROUTE_HEADS_PER_STEP, heads, 0)
    rows_ref[...] = e_ref[...].T.astype(jnp.int32) * ROW_WORDS


def _route(qp, keys_bf):
    t = qp.shape[0]
    tm = ROUTE_TILE
    out = pl.BlockSpec((PEER_SLOTS, tm), lambda i: (0, i))
    return pl.pallas_call(
        _route_kernel,
        grid=(t // tm,),
        in_specs=[pl.BlockSpec((tm, PEER_HEADS * PEER_DK), lambda i: (i, 0)),
                  _const_spec((2 * PEER_HEADS, N_KEYS, PEER_HALF))],
        out_specs=[pl.BlockSpec((tm, PEER_SLOTS), lambda i: (i, 0)), out],
        out_shape=[jax.ShapeDtypeStruct((t, PEER_SLOTS), jnp.int32),
                   jax.ShapeDtypeStruct((PEER_SLOTS, t), F32)],
        scratch_shapes=[pltpu.VMEM((PEER_SLOTS, tm), F32)],
        compiler_params=_cparams(1),
        name="peer_route",
    )(qp, keys_bf)


def _pack_table(tab):
    tb = tab.astype(BF16)
    hi = lax.bitcast_convert_type(tb[:, :512], jnp.uint16).astype(jnp.uint32)
    lo = lax.bitcast_convert_type(tb[:, 512:], jnp.uint16).astype(jnp.uint32)
    return ((hi << 16) | lo).reshape(N_EXPERTS * ROW_WORDS, 128)


def _gather_group(idx_ref, g, tab_ref, bufs):
    for u, buf in enumerate(bufs):
        for a in range(PEER_SLOTS // IDX_QUAD):
            quad = idx_ref.at[g * PEER_UNROLL + u, pl.ds(a * IDX_QUAD, IDX_QUAD)]
            for b in range(IDX_QUAD):
                k = a * IDX_QUAD + b
                row = pl.multiple_of(quad[b], ROW_WORDS)
                buf[pl.ds(k, ROW_WORDS, stride=G_STRIDE), :] = tab_ref[pl.ds(row, ROW_WORDS), :]


def _lane_tile(buf_ref, j):
    word = buf_ref[j * G_STRIDE:j * G_STRIDE + PEER_SLOTS, :]
    hi = pltpu.bitcast(word & jnp.uint32(0xFFFF0000), F32)
    lo = pltpu.bitcast(word << 16, F32)
    return hi, lo


def _token_pipeline(idx_ref, next_idx_ref, tab_ref, bufs, compute_group, split_regions):
    n_groups = PEER_TOKENS // PEER_UNROLL
    halves = (bufs[:PEER_UNROLL], bufs[PEER_UNROLL:])
    step = pl.program_id(0)

    @pl.when(step == 0)
    def _():
        _gather_group(idx_ref, 0, tab_ref, halves[0])

    def pair(g, half, src_ref, src_group):
        def run():
            _gather_group(src_ref, src_group, tab_ref, halves[1 - half])
            compute_group(g * PEER_UNROLL, halves[half])

        if split_regions:
            pl.when(step >= 0)(run)
        else:
            run()

    def body(i, _):
        for half in range(2):
            pair(2 * i + half, half, idx_ref, 2 * i + half + 1)
        return 0

    lax.fori_loop(0, n_groups // 2 - 1, body, 0)
    pair(n_groups - 2, 0, idx_ref, n_groups - 1)
    pair(n_groups - 1, 1, next_idx_ref, 0)


def _gathered_rows(buf):
    tiles = [_lane_tile(buf, j) for j in range(ROW_WORDS)]
    return jnp.concatenate([tl[0] for tl in tiles] + [tl[1] for tl in tiles], axis=1)


def _peer_u_kernel(idx_ref, next_idx_ref, tab_ref, xt_ref, g_ref, w_ref, xw_ref, act_ref, *bufs):
    lane = lax.broadcasted_iota(jnp.int32, (PEER_SLOTS, PEER_TOKENS), 1)
    xw_ref[...] = xt_ref[...].astype(F32)

    def compute_group(t0, bufs):
        for first in range(0, PEER_UNROLL, U_DOT_TOKENS):
            part = bufs[first:first + U_DOT_TOKENS]
            lhs = jnp.concatenate([_gathered_rows(b) for b in part], axis=0)
            r = jnp.dot(lhs, xw_ref[...], preferred_element_type=F32)
            for u in range(U_DOT_TOKENS):
                pltpu.store(act_ref, r[u * PEER_SLOTS:(u + 1) * PEER_SLOTS], mask=lane == t0 + first + u)

    _token_pipeline(idx_ref, next_idx_ref, tab_ref, bufs, compute_group, split_regions=True)
    act = act_ref[...]
    gelu = 0.5 * act * (1.0 + lax.erf(act * (1.0 / math.sqrt(2.0))))
    w_ref[...] = g_ref[...] * gelu


def _peer_v_kernel(idx_ref, next_idx_ref, tab_ref, wgt_ref, x_ref, o_ref, wt_ref, *bufs):
    sub = lax.broadcasted_iota(jnp.int32, (8, PEER_SLOTS), 0)
    wt_ref[...] = wgt_ref[...].T

    def compute_group(t0, bufs):
        for u, buf in enumerate(bufs):
            w = wt_ref[pl.ds(t0 + u, 1), :]
            w0 = w.astype(BF16).astype(F32)
            r1 = w - w0
            w1 = r1.astype(BF16).astype(F32)
            w2 = r1 - w1
            lhs = jnp.where(sub == 0, w0, jnp.where(sub == 1, w1, jnp.where(sub == 2, w2, 0.0)))
            r = jnp.dot(lhs, _gathered_rows(buf), preferred_element_type=F32)
            o_ref[pl.ds(t0 + u, 1), :] = x_ref[pl.ds(t0 + u, 1), :] + (r[0:1] + r[1:2] + r[2:3])

    _token_pipeline(idx_ref, next_idx_ref, tab_ref, bufs, compute_group, split_regions=False)


def _idx_specs(n_steps):
    shape = (PEER_TOKENS, PEER_SLOTS)
    return [pl.BlockSpec(shape, lambda i: (i, 0), memory_space=pltpu.SMEM),
            pl.BlockSpec(shape, lambda i: (jnp.minimum(i + 1, n_steps - 1), 0), memory_space=pltpu.SMEM)]


def _table_spec():
    return pl.BlockSpec((N_EXPERTS * ROW_WORDS, 128), lambda i: (0, 0), pipeline_mode=pl.Buffered(1))


def _gather_buffers():
    return [pltpu.VMEM((G_ROWS, 128), jnp.uint32) for _ in range(2 * PEER_UNROLL)]


def _peer_u(rows, tab, xt, g_t):
    t = xt.shape[1]
    n_steps = t // PEER_TOKENS
    col = pl.BlockSpec((PEER_SLOTS, PEER_TOKENS), lambda i: (0, i))
    return pl.pallas_call(
        _peer_u_kernel,
        grid=(n_steps,),
        in_specs=_idx_specs(n_steps) + [_table_spec(),
                                        pl.BlockSpec((D_MODEL, PEER_TOKENS), lambda i: (0, i)), col],
        out_specs=col,
        out_shape=jax.ShapeDtypeStruct((PEER_SLOTS, t), F32),
        scratch_shapes=[pltpu.VMEM((D_MODEL, PEER_TOKENS), F32),
                        pltpu.VMEM((PEER_SLOTS, PEER_TOKENS), F32)] + _gather_buffers(),
        compiler_params=_cparams(1),
        name="peer_u",
    )(rows, rows, tab, xt, g_t)


def _peer_v(rows, tab, wgt_t, x):
    t = x.shape[0]
    n_steps = t // PEER_TOKENS
    row = pl.BlockSpec((PEER_TOKENS, D_MODEL), lambda i: (i, 0))
    col = pl.BlockSpec((PEER_SLOTS, PEER_TOKENS), lambda i: (0, i))
    return pl.pallas_call(
        _peer_v_kernel,
        grid=(n_steps,),
        in_specs=_idx_specs(n_steps) + [_table_spec(), col, row],
        out_specs=row,
        out_shape=jax.ShapeDtypeStruct((t, D_MODEL), F32),
        scratch_shapes=[pltpu.VMEM((PEER_TOKENS, PEER_SLOTS), F32)] + _gather_buffers(),
        compiler_params=_cparams(1),
        name="peer_v",
    )(rows, rows, tab, wgt_t, x)


def _final_norm_kernel(x_ref, g_ref, y_ref):
    x = x_ref[...]
    ms = jnp.mean(x * x, axis=-1, keepdims=True)
    y_ref[...] = x * lax.rsqrt(ms + RMS_EPS) * g_ref[...]


def _final_norm(x, g, first_row, n_rows):
    tm = TOKEN_TILE
    base = first_row // tm
    return pl.pallas_call(
        _final_norm_kernel,
        grid=(n_rows // tm,),
        in_specs=[pl.BlockSpec((tm, D_MODEL), lambda i: (base + i, 0)), _const_spec((1, D_MODEL))],
        out_specs=pl.BlockSpec((tm, D_MODEL), lambda i: (i, 0)),
        out_shape=jax.ShapeDtypeStruct((n_rows, D_MODEL), F32),
        compiler_params=_cparams(1),
        name="final_norm",
    )(x, g)


def kernel(x_prompt, x_sample, cache_k, cache_v, state_ssm_re, state_ssm_im, norm1_g, w_in, ssm_a_re, ssm_a_im, ssm_log_dt, ssm_b_re, ssm_b_im, ssm_c_re, ssm_c_im, ssm_d, ssm_glu_w, attn_rel_bias, attn_w_o, w_out, norm2_g, peer_w_q, peer_sub_keys, peer_u, peer_v, final_g):
    bp, lp, _ = x_prompt.shape
    bs, ls, _ = x_sample.shape
    depth = w_in.shape[0]
    tp = bp * lp
    kv_win = cache_k.shape[2]
    keep = min(LEFT_CHUNKS * CHUNK, lp)
    x = jnp.concatenate([x_prompt.reshape(tp, D_MODEL), x_sample.reshape(bs * ls, D_MODEL)], axis=0)
    t = x.shape[0]
    assert t % TOKEN_TILE == 0 and lp % SCAN_CHUNK == 0 and lp % CHUNK == 0 and kv_win == LEFT_CHUNKS * CHUNK

    zeros_state = jnp.zeros((bp, N_SSM), F32)
    outs = {n: [] for n in ("p_re", "p_im", "p_k", "p_v", "s_re", "s_im", "s_k", "s_v")}
    for l in range(depth):
        u, q, kv, kvb, gates = _inproj(x, norm1_g[l][None], w_in[l].astype(BF16))

        ssm_args = (ssm_a_re[l], ssm_a_im[l], ssm_log_dt[l], ssm_b_re[l], ssm_b_im[l], ssm_c_re[l], ssm_c_im[l])
        dskip = ssm_d[l][None]
        bcat, apr, api, ccat = _ssm_params(*ssm_args, SCAN_CHUNK)
        y, hr_p, hi_p = _ssm(u, None, bp, lp, 0, zeros_state, zeros_state,
                             bcat, apr, api, ccat, dskip, SCAN_CHUNK)
        y, hr_s, hi_s = _ssm(u, y, bs, ls, tp, state_ssm_re[l].reshape(bs, N_SSM),
                             state_ssm_im[l].reshape(bs, N_SSM), bcat, apr[:ls], api[:ls], ccat, dskip, ls)

        pad = ((0, 0), (LEFT_CHUNKS * CHUNK, 0), (0, 0))
        kb_p = jnp.pad(kvb[:tp, :D_ATTN].reshape(bp, lp, D_ATTN), pad)
        vb_p = jnp.pad(kvb[:tp, D_ATTN:].reshape(bp, lp, D_ATTN), pad)
        w_p = (LEFT_CHUNKS + 1) * CHUNK
        att = _attention(q, None, bp, lp, 0, kb_p, vb_p,
                         _bias_table(attn_rel_bias[l], CHUNK, w_p), CHUNK, w_p, True)
        kb_s = jnp.concatenate([cache_k[l].reshape(bs, kv_win, D_ATTN).astype(BF16),
                                kvb[tp:, :D_ATTN].reshape(bs, ls, D_ATTN)], axis=1)
        vb_s = jnp.concatenate([cache_v[l].reshape(bs, kv_win, D_ATTN).astype(BF16),
                                kvb[tp:, D_ATTN:].reshape(bs, ls, D_ATTN)], axis=1)
        w_s = kv_win + ls
        att = _attention(q, att, bs, ls, tp, kb_s, vb_s,
                         _bias_table(attn_rel_bias[l], ls, w_s), ls, w_s, False)

        x1, xn_t, qp = _mix(x, y, att, gates, ssm_glu_w[l].astype(BF16), attn_w_o[l].astype(BF16),
                          w_out[l].astype(BF16), norm2_g[l][None], peer_w_q[l].astype(BF16))
        keys = peer_sub_keys[l].reshape(2 * PEER_HEADS, N_KEYS, PEER_HALF).astype(BF16)
        rows, g_t = _route(qp, keys)
        wgt_t = _peer_u(rows, _pack_table(peer_u[l]), xn_t, g_t)
        x = _peer_v(rows, _pack_table(peer_v[l]), wgt_t, x1)

        kv_p = jnp.stack([kv[(i + 1) * lp - keep:(i + 1) * lp] for i in range(bp)])
        kv_p = kv_p.reshape(bp, keep, 2, N_HEADS, HEAD_DIM)
        kv_s = kv[tp:].reshape(bs, ls, 2, N_HEADS, HEAD_DIM)
        outs["p_re"].append(hr_p.reshape(bp, N_GROUPS, SSM_STATE))
        outs["p_im"].append(hi_p.reshape(bp, N_GROUPS, SSM_STATE))
        outs["p_k"].append(kv_p[:, :, 0])
        outs["p_v"].append(kv_p[:, :, 1])
        outs["s_re"].append(hr_s.reshape(bs, N_GROUPS, SSM_STATE))
        outs["s_im"].append(hi_s.reshape(bs, N_GROUPS, SSM_STATE))
        outs["s_k"].append(kv_s[:, :, 0])
        outs["s_v"].append(kv_s[:, :, 1])

    y_p = _final_norm(x, final_g[None], 0, tp)
    y_s = _final_norm(x, final_g[None], tp, bs * ls)
    st = {n: jnp.stack(v) for n, v in outs.items()}
    return (y_p.reshape(bp, lp, D_MODEL), y_s.reshape(bs, ls, D_MODEL),
            st["p_re"], st["p_im"], st["p_k"], st["p_v"], st["s_re"], st["s_im"], st["s_k"], st["s_v"])
```

```python
import functools
import math

import jax
import jax.numpy as jnp
import numpy as np
from jax import lax
from jax.experimental import pallas as pl
from jax.experimental.pallas import tpu as pltpu

F32 = jnp.float32
BF16 = jnp.bfloat16

D_MODEL = 1024
CHUNK = 64
D_SSM = 512
SSM_GROUP = 16
N_GROUPS = D_SSM // SSM_GROUP
SSM_STATE = 64
N_SSM = N_GROUPS * SSM_STATE
N_HEADS = 8
HEAD_DIM = 64
D_ATTN = N_HEADS * HEAD_DIM
LEFT_CHUNKS = 8
REL_CLIP = 128
ATTN_SCALE = HEAD_DIM ** -0.5
N_KEYS = 128
N_EXPERTS = N_KEYS * N_KEYS
PEER_HEADS = 8
PEER_DK = 256
PEER_HALF = PEER_DK // 2
PEER_TOPK = 16
PEER_SLOTS = PEER_HEADS * PEER_TOPK
RMS_EPS = 1e-6
D_IN = D_SSM + 3 * D_ATTN + 2 * D_MODEL

VMEM_LIMIT_BYTES = 56 * 1024 * 1024
TOKEN_TILE = 256
ROUTE_TILE = 128
ROUTE_HEADS_PER_STEP = 8
PEER_TOKENS = 128
PEER_U_GROUP = 8
PEER_V_GROUP = 16
U_DOT_TOKENS = 4
SCAN_CHUNK = 256
SCAN_ROWS = 8
NEG_INF = float("-inf")

ROW_WORDS = 4
G_STRIDE = 136
G_ROWS = ROW_WORDS * G_STRIDE
IDX_QUAD = 8

PEER_CAND = [(i, j) for i in range(PEER_TOPK) for j in range(PEER_TOPK) if (i + 1) * (j + 1) <= PEER_TOPK]
N_CAND_ROWS = 56


def _cparams(n_axes):
    return pltpu.CompilerParams(dimension_semantics=("arbitrary",) * n_axes,
                                vmem_limit_bytes=VMEM_LIMIT_BYTES)


def _const_spec(shape):
    return pl.BlockSpec(shape, lambda *_: (0,) * len(shape))


def _inproj_kernel(x_ref, g_ref, w_ref, u_ref, q_ref, kv_ref, kvb_ref, gate_ref):
    x = x_ref[...]
    ms = jnp.mean(x * x, axis=-1, keepdims=True)
    xn = (x * lax.rsqrt(ms + RMS_EPS) * g_ref[...]).astype(BF16)

    def proj(lo, hi):
        return jnp.dot(xn, w_ref[:, lo:hi], preferred_element_type=F32)

    u_ref[...] = proj(0, D_SSM).astype(BF16)
    q_ref[...] = proj(D_SSM, D_SSM + D_ATTN).astype(BF16)
    for c in range(2):
        lo = D_SSM + D_ATTN + c * D_ATTN
        kv = proj(lo, lo + D_ATTN)
        kv_ref[:, c * D_ATTN:(c + 1) * D_ATTN] = kv
        kvb_ref[:, c * D_ATTN:(c + 1) * D_ATTN] = kv.astype(BF16)
    for c in range(4):
        lo = D_SSM + 3 * D_ATTN + c * 512
        gate_ref[:, c * 512:(c + 1) * 512] = jax.nn.sigmoid(proj(lo, lo + 512)).astype(BF16)


def _inproj(x, g, w_bf):
    t = x.shape[0]
    tm = TOKEN_TILE
    row = lambda n: pl.BlockSpec((tm, n), lambda i: (i, 0))
    return pl.pallas_call(
        _inproj_kernel,
        grid=(t // tm,),
        in_specs=[row(D_MODEL), _const_spec((1, D_MODEL)), _const_spec((D_MODEL, D_IN))],
        out_specs=[row(D_SSM), row(D_ATTN), row(2 * D_ATTN), row(2 * D_ATTN), row(2 * D_MODEL)],
        out_shape=[jax.ShapeDtypeStruct((t, D_SSM), BF16), jax.ShapeDtypeStruct((t, D_ATTN), BF16),
                   jax.ShapeDtypeStruct((t, 2 * D_ATTN), F32), jax.ShapeDtypeStruct((t, 2 * D_ATTN), BF16),
                   jax.ShapeDtypeStruct((t, 2 * D_MODEL), BF16)],
        compiler_params=_cparams(1),
        name="inproj",
    )(x, g, w_bf)


def _ssm_kernel(lc, u_ref, bcat_ref, apr_ref, api_ref, ccat_ref, d_ref, h0r_ref, h0i_ref, *rest):
    y_ref, hr_ref, hi_ref, re_ref, im_ref, cr_ref, ci_ref = rest[-7:]
    c = pl.program_id(1)

    @pl.when(c == 0)
    def _():
        cr_ref[...] = h0r_ref[0]
        ci_ref[...] = h0i_ref[0]

    u = u_ref[...]
    hc, hs = D_SSM // 2, N_SSM // 2
    for half in range(2):
        uh = u[:, half * hc:(half + 1) * hc]
        st = slice(half * hs, (half + 1) * hs)
        re_ref[:, st] = jnp.dot(uh, bcat_ref[half * hc:(half + 1) * hc, st], preferred_element_type=F32)
        im_ref[:, st] = jnp.dot(uh, bcat_ref[half * hc:(half + 1) * hc, N_SSM + half * hs:N_SSM + (half + 1) * hs],
                                preferred_element_type=F32)

    row = lax.broadcasted_iota(jnp.int32, (SCAN_ROWS, 128), 0)

    def lane_tile(j, _):
        sl = pl.ds(pl.multiple_of(j * 128, 128), 128)
        steps = []
        s = 1
        while s < SCAN_ROWS:
            steps.append((s, apr_ref[s - 1:s, sl], api_ref[s - 1:s, sl], row >= s))
            s *= 2
        pr = apr_ref[0:SCAN_ROWS, sl]
        pi = api_ref[0:SCAN_ROWS, sl]
        c_r = cr_ref[:, sl]
        c_i = ci_ref[:, sl]
        for blk in range(lc // SCAN_ROWS):
            rows = slice(blk * SCAN_ROWS, (blk + 1) * SCAN_ROWS)
            hr = re_ref[rows, sl]
            hi = im_ref[rows, sl]
            for s, ar, ai, keep in steps:
                sr = jnp.where(keep, pltpu.roll(hr, s, 0), 0.0)
                si = jnp.where(keep, pltpu.roll(hi, s, 0), 0.0)
                hr, hi = hr + (ar * sr - ai * si), hi + (ar * si + ai * sr)
            hr, hi = hr + (pr * c_r - pi * c_i), hi + (pr * c_i + pi * c_r)
            re_ref[rows, sl] = hr
            im_ref[rows, sl] = hi
            c_r = hr[SCAN_ROWS - 1:SCAN_ROWS]
            c_i = hi[SCAN_ROWS - 1:SCAN_ROWS]
        cr_ref[:, sl] = c_r
        ci_ref[:, sl] = c_i
        return 0

    lax.fori_loop(0, N_SSM // 128, lane_tile, 0)

    hr_ref[0] = cr_ref[...]
    hi_ref[0] = ci_ref[...]
    for half in range(2):
        st = slice(half * hs, (half + 1) * hs)
        ch = slice(half * hc, (half + 1) * hc)
        y = jnp.dot(re_ref[:, st].astype(BF16), ccat_ref[half * hs:(half + 1) * hs, ch],
                    preferred_element_type=F32)
        y = y + jnp.dot(im_ref[:, st].astype(BF16), ccat_ref[N_SSM + half * hs:N_SSM + (half + 1) * hs, ch],
                        preferred_element_type=F32)
        y_ref[:, ch] = (y + d_ref[:, ch] * u[:, ch].astype(F32)).astype(BF16)


def _seq_rows_spec(rows, width, first_row, seq_len):
    per_seq = seq_len // rows
    base = first_row // rows
    return pl.BlockSpec((rows, width), lambda i, n: (base + i * per_seq + n, 0))


def _alias_prev(prev, n_inputs):
    if prev is None:
        return [], [], {}
    return [prev], [pl.BlockSpec(memory_space=pl.ANY)], {n_inputs: 0}


def _ssm(u, prev_y, b, l, first_row, h0r, h0i, bcat, apr, api, ccat, dskip, lc):
    st = jax.ShapeDtypeStruct((b, 1, N_SSM), F32)
    state_spec = pl.BlockSpec((1, 1, N_SSM), lambda i, c: (i, 0, 0))
    rows = _seq_rows_spec(lc, D_SSM, first_row, l)
    extra, extra_specs, aliases = _alias_prev(prev_y, 8)
    return pl.pallas_call(
        functools.partial(_ssm_kernel, lc),
        grid=(b, l // lc),
        in_specs=[rows, _const_spec((D_SSM, 2 * N_SSM)), _const_spec((lc, N_SSM)), _const_spec((lc, N_SSM)),
                  _const_spec((2 * N_SSM, D_SSM)), _const_spec((1, D_SSM)), state_spec, state_spec] + extra_specs,
        out_specs=[rows, state_spec, state_spec],
        out_shape=[jax.ShapeDtypeStruct(u.shape, BF16), st, st],
        scratch_shapes=[pltpu.VMEM((lc, N_SSM), F32), pltpu.VMEM((lc, N_SSM), F32),
                        pltpu.VMEM((1, N_SSM), F32), pltpu.VMEM((1, N_SSM), F32)],
        input_output_aliases=aliases,
        compiler_params=_cparams(2),
        name=f"ssm_scan_{lc}",
    )(u, bcat, apr, api, ccat, dskip, h0r.reshape(b, 1, N_SSM), h0i.reshape(b, 1, N_SSM), *extra)


def _ssm_params(a_re, a_im, log_dt, b_re, b_im, c_re, c_im, lc):
    a = lax.complex(a_re, a_im)
    adt = a * jnp.exp(log_dt)[:, None]
    a_bar = jnp.exp(adt)
    b_bar = ((a_bar - 1.0) / a)[..., None] * lax.complex(b_re, b_im)
    eye = jnp.eye(N_GROUPS, dtype=F32)
    bre = jnp.einsum("gpc,gh->gchp", jnp.real(b_bar), eye).reshape(D_SSM, N_SSM)
    bim = jnp.einsum("gpc,gh->gchp", jnp.imag(b_bar), eye).reshape(D_SSM, N_SSM)
    bcat = jnp.concatenate([bre, bim], axis=1).astype(BF16)
    cre = jnp.einsum("gcp,gh->gphc", c_re, eye).reshape(N_SSM, D_SSM)
    cim = jnp.einsum("gcp,gh->gphc", c_im, eye).reshape(N_SSM, D_SSM)
    ccat = jnp.concatenate([cre, -cim], axis=0).astype(BF16)
    steps = jnp.arange(1, lc + 1, dtype=F32)[:, None, None]
    apow = jnp.exp(adt[None] * steps).reshape(lc, N_SSM)
    return bcat, jnp.real(apow), jnp.imag(apow), ccat


def _attn_kernel(qc, w, masked, q_ref, k_ref, v_ref, bias_ref, *rest):
    o_ref = rest[-1]
    n = pl.program_id(1)
    second = lax.broadcasted_iota(jnp.int32, (qc, 2 * HEAD_DIM), 1) >= HEAD_DIM
    start = pl.multiple_of(n * qc, qc)
    scores = []
    for pr in range(N_HEADS // 2):
        lanes = slice(pr * 2 * HEAD_DIM, (pr + 1) * 2 * HEAD_DIM)
        q2 = q_ref[:, lanes]
        zero = jnp.zeros_like(q2)
        qs = jnp.concatenate([jnp.where(second, zero, q2), jnp.where(second, q2, zero)], axis=0)
        k2 = k_ref[0, pl.ds(start, w), lanes]
        scores.append(lax.dot_general(qs, k2, (((1,), (1,)), ((), ())), preferred_element_type=F32))
    s = jnp.concatenate(scores, axis=0)
    s = s * ATTN_SCALE + bias_ref[...]
    if masked:
        col = lax.broadcasted_iota(jnp.int32, (N_HEADS * qc, w), 1)
        s = jnp.where(col + n * qc >= LEFT_CHUNKS * CHUNK, s, jnp.finfo(F32).min)
    m = jnp.max(s, axis=-1, keepdims=True)
    p = jnp.exp(s - m)
    p = p / jnp.sum(p, axis=-1, keepdims=True)
    p = p.astype(BF16)
    for pr in range(N_HEADS // 2):
        lanes = slice(pr * 2 * HEAD_DIM, (pr + 1) * 2 * HEAD_DIM)
        v2 = v_ref[0, pl.ds(start, w), lanes]
        r = jnp.dot(p[2 * pr * qc:(2 * pr + 2) * qc], v2, preferred_element_type=F32)
        o_ref[:, lanes] = jnp.where(second, r[qc:], r[:qc]).astype(BF16)


def _attention(q, prev_att, b, l, first_row, k, v, bias, qc, w, masked):
    lk = k.shape[1]
    kv_spec = pl.BlockSpec((1, lk, D_ATTN), lambda i, n: (i, 0, 0))
    rows = _seq_rows_spec(qc, D_ATTN, first_row, l)
    extra, extra_specs, aliases = _alias_prev(prev_att, 4)
    return pl.pallas_call(
        functools.partial(_attn_kernel, qc, w, masked),
        grid=(b, l // qc),
        in_specs=[rows, kv_spec, kv_spec, _const_spec((N_HEADS * qc, w))] + extra_specs,
        out_specs=rows,
        out_shape=jax.ShapeDtypeStruct(q.shape, BF16),
        input_output_aliases=aliases,
        compiler_params=_cparams(2),
        name=f"band_attn_{qc}",
    )(q, k, v, bias, *extra)


def _bias_table(rel_bias, qc, w):
    lo = LEFT_CHUNKS * CHUNK + (qc - 1) - REL_CLIP
    hi = (w + qc - 1) - lo - (2 * REL_CLIP + 1)
    ext = jnp.pad(rel_bias.astype(F32), ((0, 0), (lo, max(hi, 0))), mode="edge")
    rows = [ext[:, qc - 1 - i:qc - 1 - i + w] for i in range(qc)]
    return jnp.stack(rows, axis=1).reshape(N_HEADS * qc, w)


def _mix_kernel(x_ref, y_ref, att_ref, gate_ref, glu_ref, wo_ref, wout_ref, g2_ref, wq_ref,
                x1_ref, xnt_ref, qp_ref):
    glu = jnp.dot(y_ref[...], glu_ref[...], preferred_element_type=F32)
    a = glu[:, :D_MODEL] * jax.nn.sigmoid(glu[:, D_MODEL:])
    b = jnp.dot(att_ref[...], wo_ref[...], preferred_element_type=F32)
    mixed = gate_ref[:, :D_MODEL].astype(F32) * a + gate_ref[:, D_MODEL:].astype(F32) * b
    x1 = x_ref[...] + jnp.dot(mixed.astype(BF16), wout_ref[...], preferred_element_type=F32)
    x1_ref[...] = x1
    ms = jnp.mean(x1 * x1, axis=-1, keepdims=True)
    xn = x1 * lax.rsqrt(ms + RMS_EPS) * g2_ref[...]
    xnt_ref[...] = xn.T.astype(BF16)
    qp_ref[...] = jnp.dot(xn.astype(BF16), wq_ref[...], preferred_element_type=F32).astype(BF16)


def _mix(x, y, att, gates, glu_w, w_o, w_out, g2, w_q):
    t = x.shape[0]
    tm = TOKEN_TILE
    row = lambda n: pl.BlockSpec((tm, n), lambda i: (i, 0))
    dq = PEER_HEADS * PEER_DK
    return pl.pallas_call(
        _mix_kernel,
        grid=(t // tm,),
        in_specs=[row(D_MODEL), row(D_SSM), row(D_ATTN), row(2 * D_MODEL),
                  _const_spec((D_SSM, 2 * D_MODEL)), _const_spec((D_ATTN, D_MODEL)),
                  _const_spec((D_MODEL, D_MODEL)), _const_spec((1, D_MODEL)), _const_spec((D_MODEL, dq))],
        out_specs=[row(D_MODEL), pl.BlockSpec((D_MODEL, tm), lambda i: (0, i)), row(dq)],
        out_shape=[jax.ShapeDtypeStruct((t, D_MODEL), F32), jax.ShapeDtypeStruct((D_MODEL, t), BF16),
                   jax.ShapeDtypeStruct((t, dq), BF16)],
        compiler_params=_cparams(1),
        name="mix",
    )(x, y, att, gates, glu_w, w_o, w_out, g2, w_q)


def _extract_max(s, row):
    m = jnp.max(s, axis=0, keepdims=True)
    pos = jnp.min(jnp.where(s == m, row, float(s.shape[0])), axis=0, keepdims=True)
    return m, pos, jnp.where(row == pos, NEG_INF, s)


def _top_keys(s, key_ids):
    n = N_KEYS // 4
    v = [s[i * n:(i + 1) * n] for i in range(4)]
    ids = list(key_ids)

    def order(a, b, ids_ordered):
        first = v[a] >= v[b] if ids_ordered else (v[a] > v[b]) | ((v[a] == v[b]) & (ids[a] < ids[b]))
        v[a], v[b] = jnp.where(first, v[a], v[b]), jnp.where(first, v[b], v[a])
        ids[a], ids[b] = jnp.where(first, ids[a], ids[b]), jnp.where(first, ids[b], ids[a])

    order(0, 1, True)
    order(2, 3, True)
    order(0, 2, True)
    order(1, 3, True)
    order(1, 2, False)
    top_v, top_i = [], []
    for _ in range(PEER_TOPK):
        m = jnp.max(v[0], axis=0, keepdims=True)
        key = jnp.min(jnp.where(v[0] == m, ids[0], float(N_KEYS)), axis=0, keepdims=True)
        won = ids[0] == key
        for level in range(3):
            v[level] = jnp.where(won, v[level + 1], v[level])
            ids[level] = jnp.where(won, ids[level + 1], ids[level])
        v[3] = jnp.where(won, NEG_INF, v[3])
        top_v.append(m)
        top_i.append(key)
    return top_v, top_i


def _route_kernel(qp_ref, keys_ref, rows_ref, g_ref, e_ref):
    tm = ROUTE_TILE
    quarter = lax.broadcasted_iota(jnp.int32, (N_KEYS // 4, tm), 0).astype(F32)
    key_ids = [quarter + float(i * (N_KEYS // 4)) for i in range(4)]
    row_c =lax.broadcasted_iota(jnp.int32, (N_CAND_ROWS, tm), 0).astype(F32)
    sub_masks = [lax.broadcasted_iota(jnp.int32, (8, tm), 0) == r for r in range(8)]

    def stack_rows(rows, n_rows, fill):
        groups = []
        for first in range(0, n_rows, 8):
            acc = jnp.full((8, tm), fill, F32)
            for r, value in enumerate(rows[first:first + 8]):
                acc = jnp.where(sub_masks[r], value, acc)
            groups.append(acc)
        return jnp.concatenate(groups, axis=0)

    def head(h, _):
        vals, ids = [], []
        for z in range(2):
            hz = 2 * h + z
            q = qp_ref[:, pl.ds(pl.multiple_of(hz * PEER_HALF, PEER_HALF), PEER_HALF)]
            s = lax.dot_general(keys_ref[hz], q, (((1,), (1,)), ((), ())),
                                preferred_element_type=F32)
            v_z, i_z = _top_keys(s, key_ids)
            vals.append(v_z)
            ids.append(i_z)
        cand = stack_rows([vals[0][i] + vals[1][j] for i, j in PEER_CAND], N_CAND_ROWS, NEG_INF)
        cand_e = stack_rows([ids[0][i] * float(N_KEYS) + ids[1][j] for i, j in PEER_CAND], N_CAND_ROWS, 0.0)
        best, best_e = [], []
        top = None
        for _ in range(PEER_TOPK):
            m, pos, cand = _extract_max(cand, row_c)
            top = m if top is None else top
            best.append(jnp.exp(m - top))
            best_e.append(jnp.max(jnp.where(row_c == pos, cand_e, -1.0), axis=0, keepdims=True))
        best = stack_rows(best, PEER_TOPK, 0.0)
        out_rows = pl.ds(pl.multiple_of(h * PEER_TOPK, PEER_TOPK), PEER_TOPK)
        e_ref[out_rows, :] = stack_rows(best_e, PEER_TOPK, 0.0)
        g_ref[out_rows, :] = best / jnp.sum(best, axis=0, keepdims=True)
        return 0

    def heads(i, _):
        for u in range(ROUTE_HEADS_PER_STEP):
            head(i * ROUTE_HEADS_PER_STEP + u, 0)
        return 0

    lax.fori_loop(0, PEER_HEADS // ROUTE_HEADS_PER_STEP, heads, 0)
    rows_ref[...] = e_ref[...].T.astype(jnp.int32) * ROW_WORDS


def _route(qp, keys_bf):
    t = qp.shape[0]
    tm = ROUTE_TILE
    out = pl.BlockSpec((PEER_SLOTS, tm), lambda i: (0, i))
    return pl.pallas_call(
        _route_kernel,
        grid=(t // tm,),
        in_specs=[pl.BlockSpec((tm, PEER_HEADS * PEER_DK), lambda i: (i, 0)),
                  _const_spec((2 * PEER_HEADS, N_KEYS, PEER_HALF))],
        out_specs=[pl.BlockSpec((tm, PEER_SLOTS), lambda i: (i, 0)), out],
        out_shape=[jax.ShapeDtypeStruct((t, PEER_SLOTS), jnp.int32),
                   jax.ShapeDtypeStruct((PEER_SLOTS, t), F32)],
        scratch_shapes=[pltpu.VMEM((PEER_SLOTS, tm), F32)],
        compiler_params=_cparams(1),
        name="peer_route",
    )(qp, keys_bf)


def _pack_table(tab):
    tb = tab.astype(BF16)
    hi = lax.bitcast_convert_type(tb[:, :512], jnp.uint16).astype(jnp.uint32)
    lo = lax.bitcast_convert_type(tb[:, 512:], jnp.uint16).astype(jnp.uint32)
    return ((hi << 16) | lo).reshape(N_EXPERTS * ROW_WORDS, 128)


def _gather_group(idx_ref, g, tab_ref, bufs):
    for u, buf in enumerate(bufs):
        for a in range(PEER_SLOTS // IDX_QUAD):
            quad = idx_ref.at[g * len(bufs) + u, pl.ds(a * IDX_QUAD, IDX_QUAD)]
            for b in range(IDX_QUAD):
                k = a * IDX_QUAD + b
                row = pl.multiple_of(quad[b], ROW_WORDS)
                buf[pl.ds(k, ROW_WORDS, stride=G_STRIDE), :] = tab_ref[pl.ds(row, ROW_WORDS), :]


def _lane_tile(buf_ref, j):
    word = buf_ref[j * G_STRIDE:j * G_STRIDE + PEER_SLOTS, :]
    hi = pltpu.bitcast(word & jnp.uint32(0xFFFF0000), F32)
    lo = pltpu.bitcast(word << 16, F32)
    return hi, lo


def _token_pipeline(idx_ref, next_idx_ref, tab_ref, bufs, compute_group, split_regions):
    group = len(bufs) // 2
    n_groups = PEER_TOKENS // group
    halves = (bufs[:group], bufs[group:])
    step = pl.program_id(0)

    @pl.when(step == 0)
    def _():
        _gather_group(idx_ref, 0, tab_ref, halves[0])

    def pair(g, half, src_ref, src_group):
        def run():
            _gather_group(src_ref, src_group, tab_ref, halves[1 - half])
            compute_group(g * group, halves[half])

        if split_regions:
            pl.when(step >= 0)(run)
        else:
            run()

    def body(i, _):
        for half in range(2):
            pair(2 * i + half, half, idx_ref, 2 * i + half + 1)
        return 0

    lax.fori_loop(0, n_groups // 2 - 1, body, 0)
    pair(n_groups - 2, 0, idx_ref, n_groups - 1)
    pair(n_groups - 1, 1, next_idx_ref, 0)


def _gathered_rows(buf):
    tiles = [_lane_tile(buf, j) for j in range(ROW_WORDS)]
    return jnp.concatenate([tl[0] for tl in tiles] + [tl[1] for tl in tiles], axis=1)


def _peer_u_kernel(idx_ref, next_idx_ref, tab_ref, xt_ref, g_ref, w_ref, xw_ref, act_ref, *bufs):
    lane = lax.broadcasted_iota(jnp.int32, (PEER_SLOTS, PEER_TOKENS), 1)
    xw_ref[...] = xt_ref[...].astype(F32)

    def compute_group(t0, bufs):
        for first in range(0, len(bufs), U_DOT_TOKENS):
            part = bufs[first:first + U_DOT_TOKENS]
            lhs = jnp.concatenate([_gathered_rows(b) for b in part], axis=0)
            r = jnp.dot(lhs, xw_ref[...], preferred_element_type=F32)
            for u in range(U_DOT_TOKENS):
                pltpu.store(act_ref, r[u * PEER_SLOTS:(u + 1) * PEER_SLOTS], mask=lane == t0 + first + u)

    _token_pipeline(idx_ref, next_idx_ref, tab_ref, bufs, compute_group, split_regions=True)
    act = act_ref[...]
    gelu = 0.5 * act * (1.0 + lax.erf(act * (1.0 / math.sqrt(2.0))))
    w_ref[...] = g_ref[...] * gelu


def _peer_v_kernel(idx_ref, next_idx_ref, tab_ref, wgt_ref, x_ref, o_ref, wt_ref, *bufs):
    sub = lax.broadcasted_iota(jnp.int32, (8, PEER_SLOTS), 0)
    wt_ref[...] = wgt_ref[...].T

    def compute_group(t0, bufs):
        for u, buf in enumerate(bufs):
            w = wt_ref[pl.ds(t0 + u, 1), :]
            w0 = w.astype(BF16).astype(F32)
            r1 = w - w0
            w1 = r1.astype(BF16).astype(F32)
            w2 = r1 - w1
            lhs = jnp.where(sub == 0, w0, jnp.where(sub == 1, w1, jnp.where(sub == 2, w2, 0.0)))
            r = jnp.dot(lhs, _gathered_rows(buf), preferred_element_type=F32)
            o_ref[pl.ds(t0 + u, 1), :] = x_ref[pl.ds(t0 + u, 1), :] + (r[0:1] + r[1:2] + r[2:3])

    _token_pipeline(idx_ref, next_idx_ref, tab_ref, bufs, compute_group, split_regions=False)


def _idx_specs(n_steps):
    shape = (PEER_TOKENS, PEER_SLOTS)
    return [pl.BlockSpec(shape, lambda i: (i, 0), memory_space=pltpu.SMEM),
            pl.BlockSpec(shape, lambda i: (jnp.minimum(i + 1, n_steps - 1), 0), memory_space=pltpu.SMEM)]


def _table_spec():
    return pl.BlockSpec((N_EXPERTS * ROW_WORDS, 128), lambda i: (0, 0), pipeline_mode=pl.Buffered(1))


def _gather_buffers(group):
    return [pltpu.VMEM((G_ROWS, 128), jnp.uint32) for _ in range(2 * group)]


def _peer_u(rows, tab, xt, g_t):
    t = xt.shape[1]
    n_steps = t // PEER_TOKENS
    col = pl.BlockSpec((PEER_SLOTS, PEER_TOKENS), lambda i: (0, i))
    return pl.pallas_call(
        _peer_u_kernel,
        grid=(n_steps,),
        in_specs=_idx_specs(n_steps) + [_table_spec(),
                                        pl.BlockSpec((D_MODEL, PEER_TOKENS), lambda i: (0, i)), col],
        out_specs=col,
        out_shape=jax.ShapeDtypeStruct((PEER_SLOTS, t), F32),
        scratch_shapes=[pltpu.VMEM((D_MODEL, PEER_TOKENS), F32),
                        pltpu.VMEM((PEER_SLOTS, PEER_TOKENS), F32)] + _gather_buffers(PEER_U_GROUP),
        compiler_params=_cparams(1),
        name="peer_u",
    )(rows, rows, tab, xt, g_t)


def _peer_v(rows, tab, wgt_t, x):
    t = x.shape[0]
    n_steps = t // PEER_TOKENS
    row = pl.BlockSpec((PEER_TOKENS, D_MODEL), lambda i: (i, 0))
    col = pl.BlockSpec((PEER_SLOTS, PEER_TOKENS), lambda i: (0, i))
    return pl.pallas_call(
        _peer_v_kernel,
        grid=(n_steps,),
        in_specs=_idx_specs(n_steps) + [_table_spec(), col, row],
        out_specs=row,
        out_shape=jax.ShapeDtypeStruct((t, D_MODEL), F32),
        scratch_shapes=[pltpu.VMEM((PEER_TOKENS, PEER_SLOTS), F32)] + _gather_buffers(PEER_V_GROUP),
        compiler_params=_cparams(1),
        name="peer_v",
    )(rows, rows, tab, wgt_t, x)


def _final_norm_kernel(x_ref, g_ref, y_ref):
    x = x_ref[...]
    ms = jnp.mean(x * x, axis=-1, keepdims=True)
    y_ref[...] = x * lax.rsqrt(ms + RMS_EPS) * g_ref[...]


def _final_norm(x, g, first_row, n_rows):
    tm = TOKEN_TILE
    base = first_row // tm
    return pl.pallas_call(
        _final_norm_kernel,
        grid=(n_rows // tm,),
        in_specs=[pl.BlockSpec((tm, D_MODEL), lambda i: (base + i, 0)), _const_spec((1, D_MODEL))],
        out_specs=pl.BlockSpec((tm, D_MODEL), lambda i: (i, 0)),
        out_shape=jax.ShapeDtypeStruct((n_rows, D_MODEL), F32),
        compiler_params=_cparams(1),
        name="final_norm",
    )(x, g)


def kernel(x_prompt, x_sample, cache_k, cache_v, state_ssm_re, state_ssm_im, norm1_g, w_in, ssm_a_re, ssm_a_im, ssm_log_dt, ssm_b_re, ssm_b_im, ssm_c_re, ssm_c_im, ssm_d, ssm_glu_w, attn_rel_bias, attn_w_o, w_out, norm2_g, peer_w_q, peer_sub_keys, peer_u, peer_v, final_g):
    bp, lp, _ = x_prompt.shape
    bs, ls, _ = x_sample.shape
    depth = w_in.shape[0]
    tp = bp * lp
    kv_win = cache_k.shape[2]
    keep = min(LEFT_CHUNKS * CHUNK, lp)
    x = jnp.concatenate([x_prompt.reshape(tp, D_MODEL), x_sample.reshape(bs * ls, D_MODEL)], axis=0)
    t = x.shape[0]
    assert t % TOKEN_TILE == 0 and lp % SCAN_CHUNK == 0 and lp % CHUNK == 0 and kv_win == LEFT_CHUNKS * CHUNK

    zeros_state = jnp.zeros((bp, N_SSM), F32)
    outs = {n: [] for n in ("p_re", "p_im", "p_k", "p_v", "s_re", "s_im", "s_k", "s_v")}
    for l in range(depth):
        u, q, kv, kvb, gates = _inproj(x, norm1_g[l][None], w_in[l].astype(BF16))

        ssm_args = (ssm_a_re[l], ssm_a_im[l], ssm_log_dt[l], ssm_b_re[l], ssm_b_im[l], ssm_c_re[l], ssm_c_im[l])
        dskip = ssm_d[l][None]
        bcat, apr, api, ccat = _ssm_params(*ssm_args, SCAN_CHUNK)
        y, hr_p, hi_p = _ssm(u, None, bp, lp, 0, zeros_state, zeros_state,
                             bcat, apr, api, ccat, dskip, SCAN_CHUNK)
        y, hr_s, hi_s = _ssm(u, y, bs, ls, tp, state_ssm_re[l].reshape(bs, N_SSM),
                             state_ssm_im[l].reshape(bs, N_SSM), bcat, apr[:ls], api[:ls], ccat, dskip, ls)

        pad = ((0, 0), (LEFT_CHUNKS * CHUNK, 0), (0, 0))
        kb_p = jnp.pad(kvb[:tp, :D_ATTN].reshape(bp, lp, D_ATTN), pad)
        vb_p = jnp.pad(kvb[:tp, D_ATTN:].reshape(bp, lp, D_ATTN), pad)
        w_p = (LEFT_CHUNKS + 1) * CHUNK
        att = _attention(q, None, bp, lp, 0, kb_p, vb_p,
                         _bias_table(attn_rel_bias[l], CHUNK, w_p), CHUNK, w_p, True)
        kb_s = jnp.concatenate([cache_k[l].reshape(bs, kv_win, D_ATTN).astype(BF16),
                                kvb[tp:, :D_ATTN].reshape(bs, ls, D_ATTN)], axis=1)
        vb_s = jnp.concatenate([cache_v[l].reshape(bs, kv_win, D_ATTN).astype(BF16),
                                kvb[tp:, D_ATTN:].reshape(bs, ls, D_ATTN)], axis=1)
        w_s = kv_win + ls
        att = _attention(q, att, bs, ls, tp, kb_s, vb_s,
                         _bias_table(attn_rel_bias[l], ls, w_s), ls, w_s, False)

        x1, xn_t, qp = _mix(x, y, att, gates, ssm_glu_w[l].astype(BF16), attn_w_o[l].astype(BF16),
                          w_out[l].astype(BF16), norm2_g[l][None], peer_w_q[l].astype(BF16))
        keys = peer_sub_keys[l].reshape(2 * PEER_HEADS, N_KEYS, PEER_HALF).astype(BF16)
        rows, g_t = _route(qp, keys)
        wgt_t = _peer_u(rows, _pack_table(peer_u[l]), xn_t, g_t)
        x = _peer_v(rows, _pack_table(peer_v[l]), wgt_t, x1)

        kv_p = jnp.stack([kv[(i + 1) * lp - keep:(i + 1) * lp] for i in range(bp)])
        kv_p = kv_p.reshape(bp, keep, 2, N_HEADS, HEAD_DIM)
        kv_s = kv[tp:].reshape(bs, ls, 2, N_HEADS, HEAD_DIM)
        outs["p_re"].append(hr_p.reshape(bp, N_GROUPS, SSM_STATE))
        outs["p_im"].append(hi_p.reshape(bp, N_GROUPS, SSM_STATE))
        outs["p_k"].append(kv_p[:, :, 0])
        outs["p_v"].append(kv_p[:, :, 1])
        outs["s_re"].append(hr_s.reshape(bs, N_GROUPS, SSM_STATE))
        outs["s_im"].append(hi_s.reshape(bs, N_GROUPS, SSM_STATE))
        outs["s_k"].append(kv_s[:, :, 0])
        outs["s_v"].append(kv_s[:, :, 1])

    y_p = _final_norm(x, final_g[None], 0, tp)
    y_s = _final_norm(x, final_g[None], tp, bs * ls)
    st = {n: jnp.stack(v) for n, v in outs.items()}
    return (y_p.reshape(bp, lp, D_MODEL), y_s.reshape(bs, ls, D_MODEL),
            st["p_re"], st["p_im"], st["p_k"], st["p_v"], st["s_re"], st["s_im"], st["s_k"], st["s_v"])
```

```python
import functools
import math

import jax
import jax.numpy as jnp
import numpy as np
from jax import lax
from jax.experimental import pallas as pl
from jax.experimental.pallas import tpu as pltpu

F32 = jnp.float32
BF16 = jnp.bfloat16

D_MODEL = 1024
CHUNK = 64
D_SSM = 512
SSM_GROUP = 16
N_GROUPS = D_SSM // SSM_GROUP
SSM_STATE = 64
N_SSM = N_GROUPS * SSM_STATE
N_HEADS = 8
HEAD_DIM = 64
D_ATTN = N_HEADS * HEAD_DIM
LEFT_CHUNKS = 8
REL_CLIP = 128
ATTN_SCALE = HEAD_DIM ** -0.5
N_KEYS = 128
N_EXPERTS = N_KEYS * N_KEYS
PEER_HEADS = 8
PEER_DK = 256
PEER_HALF = PEER_DK // 2
PEER_TOPK = 16
PEER_SLOTS = PEER_HEADS * PEER_TOPK
RMS_EPS = 1e-6
D_IN = D_SSM + 3 * D_ATTN + 2 * D_MODEL

VMEM_LIMIT_BYTES = 56 * 1024 * 1024
TOKEN_TILE = 256
ROUTE_TILE = 128
ROUTE_HEADS_PER_STEP = 8
PEER_TOKENS = 128
PEER_U_GROUP = 8
PEER_V_GROUP = 16
U_DOT_TOKENS = 4
SCAN_CHUNK = 256
SCAN_ROWS = 8
NEG_INF = float("-inf")

ROW_WORDS = 4
G_STRIDE = 136
G_ROWS = ROW_WORDS * G_STRIDE
IDX_QUAD = 8

PEER_CAND = [(i, j) for i in range(PEER_TOPK) for j in range(PEER_TOPK) if (i + 1) * (j + 1) <= PEER_TOPK]
N_CAND_ROWS = 64


def _cparams(n_axes):
    return pltpu.CompilerParams(dimension_semantics=("arbitrary",) * n_axes,
                                vmem_limit_bytes=VMEM_LIMIT_BYTES)


def _const_spec(shape):
    return pl.BlockSpec(shape, lambda *_: (0,) * len(shape))


def _inproj_kernel(x_ref, g_ref, w_ref, u_ref, q_ref, kv_ref, kvb_ref, gate_ref):
    x = x_ref[...]
    ms = jnp.mean(x * x, axis=-1, keepdims=True)
    xn = (x * lax.rsqrt(ms + RMS_EPS) * g_ref[...]).astype(BF16)

    def proj(lo, hi):
        return jnp.dot(xn, w_ref[:, lo:hi], preferred_element_type=F32)

    u_ref[...] = proj(0, D_SSM).astype(BF16)
    q_ref[...] = proj(D_SSM, D_SSM + D_ATTN).astype(BF16)
    for c in range(2):
        lo = D_SSM + D_ATTN + c * D_ATTN
        kv = proj(lo, lo + D_ATTN)
        kv_ref[:, c * D_ATTN:(c + 1) * D_ATTN] = kv
        kvb_ref[:, c * D_ATTN:(c + 1) * D_ATTN] = kv.astype(BF16)
    for c in range(4):
        lo = D_SSM + 3 * D_ATTN + c * 512
        gate_ref[:, c * 512:(c + 1) * 512] = jax.nn.sigmoid(proj(lo, lo + 512)).astype(BF16)


def _inproj(x, g, w_bf):
    t = x.shape[0]
    tm = TOKEN_TILE
    row = lambda n: pl.BlockSpec((tm, n), lambda i: (i, 0))
    return pl.pallas_call(
        _inproj_kernel,
        grid=(t // tm,),
        in_specs=[row(D_MODEL), _const_spec((1, D_MODEL)), _const_spec((D_MODEL, D_IN))],
        out_specs=[row(D_SSM), row(D_ATTN), row(2 * D_ATTN), row(2 * D_ATTN), row(2 * D_MODEL)],
        out_shape=[jax.ShapeDtypeStruct((t, D_SSM), BF16), jax.ShapeDtypeStruct((t, D_ATTN), BF16),
                   jax.ShapeDtypeStruct((t, 2 * D_ATTN), F32), jax.ShapeDtypeStruct((t, 2 * D_ATTN), BF16),
                   jax.ShapeDtypeStruct((t, 2 * D_MODEL), BF16)],
        compiler_params=_cparams(1),
        name="inproj",
    )(x, g, w_bf)


def _ssm_kernel(lc, u_ref, bcat_ref, apr_ref, api_ref, ccat_ref, d_ref, h0r_ref, h0i_ref, *rest):
    y_ref, hr_ref, hi_ref, re_ref, im_ref, cr_ref, ci_ref = rest[-7:]
    c = pl.program_id(1)

    @pl.when(c == 0)
    def _():
        cr_ref[...] = h0r_ref[0]
        ci_ref[...] = h0i_ref[0]

    u = u_ref[...]
    hc, hs = D_SSM // 2, N_SSM // 2
    for half in range(2):
        uh = u[:, half * hc:(half + 1) * hc]
        st = slice(half * hs, (half + 1) * hs)
        re_ref[:, st] = jnp.dot(uh, bcat_ref[half * hc:(half + 1) * hc, st], preferred_element_type=F32)
        im_ref[:, st] = jnp.dot(uh, bcat_ref[half * hc:(half + 1) * hc, N_SSM + half * hs:N_SSM + (half + 1) * hs],
                                preferred_element_type=F32)

    row = lax.broadcasted_iota(jnp.int32, (SCAN_ROWS, 128), 0)

    def lane_tile(j, _):
        sl = pl.ds(pl.multiple_of(j * 128, 128), 128)
        steps = []
        s = 1
        while s < SCAN_ROWS:
            steps.append((s, apr_ref[s - 1:s, sl], api_ref[s - 1:s, sl], row >= s))
            s *= 2
        pr = apr_ref[0:SCAN_ROWS, sl]
        pi = api_ref[0:SCAN_ROWS, sl]
        c_r = cr_ref[:, sl]
        c_i = ci_ref[:, sl]
        for blk in range(lc // SCAN_ROWS):
            rows = slice(blk * SCAN_ROWS, (blk + 1) * SCAN_ROWS)
            hr = re_ref[rows, sl]
            hi = im_ref[rows, sl]
            for s, ar, ai, keep in steps:
                sr = jnp.where(keep, pltpu.roll(hr, s, 0), 0.0)
                si = jnp.where(keep, pltpu.roll(hi, s, 0), 0.0)
                hr, hi = hr + (ar * sr - ai * si), hi + (ar * si + ai * sr)
            hr, hi = hr + (pr * c_r - pi * c_i), hi + (pr * c_i + pi * c_r)
            re_ref[rows, sl] = hr
            im_ref[rows, sl] = hi
            c_r = hr[SCAN_ROWS - 1:SCAN_ROWS]
            c_i = hi[SCAN_ROWS - 1:SCAN_ROWS]
        cr_ref[:, sl] = c_r
        ci_ref[:, sl] = c_i
        return 0

    lax.fori_loop(0, N_SSM // 128, lane_tile, 0)

    hr_ref[0] = cr_ref[...]
    hi_ref[0] = ci_ref[...]
    for half in range(2):
        st = slice(half * hs, (half + 1) * hs)
        ch = slice(half * hc, (half + 1) * hc)
        y = jnp.dot(re_ref[:, st].astype(BF16), ccat_ref[half * hs:(half + 1) * hs, ch],
                    preferred_element_type=F32)
        y = y + jnp.dot(im_ref[:, st].astype(BF16), ccat_ref[N_SSM + half * hs:N_SSM + (half + 1) * hs, ch],
                        preferred_element_type=F32)
        y_ref[:, ch] = (y + d_ref[:, ch] * u[:, ch].astype(F32)).astype(BF16)


def _seq_rows_spec(rows, width, first_row, seq_len):
    per_seq = seq_len // rows
    base = first_row // rows
    return pl.BlockSpec((rows, width), lambda i, n: (base + i * per_seq + n, 0))


def _alias_prev(prev, n_inputs):
    if prev is None:
        return [], [], {}
    return [prev], [pl.BlockSpec(memory_space=pl.ANY)], {n_inputs: 0}


def _ssm(u, prev_y, b, l, first_row, h0r, h0i, bcat, apr, api, ccat, dskip, lc):
    st = jax.ShapeDtypeStruct((b, 1, N_SSM), F32)
    state_spec = pl.BlockSpec((1, 1, N_SSM), lambda i, c: (i, 0, 0))
    rows = _seq_rows_spec(lc, D_SSM, first_row, l)
    extra, extra_specs, aliases = _alias_prev(prev_y, 8)
    return pl.pallas_call(
        functools.partial(_ssm_kernel, lc),
        grid=(b, l // lc),
        in_specs=[rows, _const_spec((D_SSM, 2 * N_SSM)), _const_spec((lc, N_SSM)), _const_spec((lc, N_SSM)),
                  _const_spec((2 * N_SSM, D_SSM)), _const_spec((1, D_SSM)), state_spec, state_spec] + extra_specs,
        out_specs=[rows, state_spec, state_spec],
        out_shape=[jax.ShapeDtypeStruct(u.shape, BF16), st, st],
        scratch_shapes=[pltpu.VMEM((lc, N_SSM), F32), pltpu.VMEM((lc, N_SSM), F32),
                        pltpu.VMEM((1, N_SSM), F32), pltpu.VMEM((1, N_SSM), F32)],
        input_output_aliases=aliases,
        compiler_params=_cparams(2),
        name=f"ssm_scan_{lc}",
    )(u, bcat, apr, api, ccat, dskip, h0r.reshape(b, 1, N_SSM), h0i.reshape(b, 1, N_SSM), *extra)


def _ssm_params(a_re, a_im, log_dt, b_re, b_im, c_re, c_im, lc):
    a = lax.complex(a_re, a_im)
    adt = a * jnp.exp(log_dt)[:, None]
    a_bar = jnp.exp(adt)
    b_bar = ((a_bar - 1.0) / a)[..., None] * lax.complex(b_re, b_im)
    eye = jnp.eye(N_GROUPS, dtype=F32)
    bre = jnp.einsum("gpc,gh->gchp", jnp.real(b_bar), eye).reshape(D_SSM, N_SSM)
    bim = jnp.einsum("gpc,gh->gchp", jnp.imag(b_bar), eye).reshape(D_SSM, N_SSM)
    bcat = jnp.concatenate([bre, bim], axis=1).astype(BF16)
    cre = jnp.einsum("gcp,gh->gphc", c_re, eye).reshape(N_SSM, D_SSM)
    cim = jnp.einsum("gcp,gh->gphc", c_im, eye).reshape(N_SSM, D_SSM)
    ccat = jnp.concatenate([cre, -cim], axis=0).astype(BF16)
    steps = jnp.arange(1, lc + 1, dtype=F32)[:, None, None]
    apow = jnp.exp(adt[None] * steps).reshape(lc, N_SSM)
    return bcat, jnp.real(apow), jnp.imag(apow), ccat


def _attn_kernel(qc, w, masked, q_ref, k_ref, v_ref, bias_ref, *rest):
    o_ref = rest[-1]
    n = pl.program_id(1)
    second = lax.broadcasted_iota(jnp.int32, (qc, 2 * HEAD_DIM), 1) >= HEAD_DIM
    start = pl.multiple_of(n * qc, qc)
    scores = []
    for pr in range(N_HEADS // 2):
        lanes = slice(pr * 2 * HEAD_DIM, (pr + 1) * 2 * HEAD_DIM)
        q2 = q_ref[:, lanes]
        zero = jnp.zeros_like(q2)
        qs = jnp.concatenate([jnp.where(second, zero, q2), jnp.where(second, q2, zero)], axis=0)
        k2 = k_ref[0, pl.ds(start, w), lanes]
        scores.append(lax.dot_general(qs, k2, (((1,), (1,)), ((), ())), preferred_element_type=F32))
    s = jnp.concatenate(scores, axis=0)
    s = s * ATTN_SCALE + bias_ref[...]
    if masked:
        col = lax.broadcasted_iota(jnp.int32, (N_HEADS * qc, w), 1)
        s = jnp.where(col + n * qc >= LEFT_CHUNKS * CHUNK, s, jnp.finfo(F32).min)
    m = jnp.max(s, axis=-1, keepdims=True)
    p = jnp.exp(s - m)
    p = p / jnp.sum(p, axis=-1, keepdims=True)
    p = p.astype(BF16)
    for pr in range(N_HEADS // 2):
        lanes = slice(pr * 2 * HEAD_DIM, (pr + 1) * 2 * HEAD_DIM)
        v2 = v_ref[0, pl.ds(start, w), lanes]
        r = jnp.dot(p[2 * pr * qc:(2 * pr + 2) * qc], v2, preferred_element_type=F32)
        o_ref[:, lanes] = jnp.where(second, r[qc:], r[:qc]).astype(BF16)


def _attention(q, prev_att, b, l, first_row, k, v, bias, qc, w, masked):
    lk = k.shape[1]
    kv_spec = pl.BlockSpec((1, lk, D_ATTN), lambda i, n: (i, 0, 0))
    rows = _seq_rows_spec(qc, D_ATTN, first_row, l)
    extra, extra_specs, aliases = _alias_prev(prev_att, 4)
    return pl.pallas_call(
        functools.partial(_attn_kernel, qc, w, masked),
        grid=(b, l // qc),
        in_specs=[rows, kv_spec, kv_spec, _const_spec((N_HEADS * qc, w))] + extra_specs,
        out_specs=rows,
        out_shape=jax.ShapeDtypeStruct(q.shape, BF16),
        input_output_aliases=aliases,
        compiler_params=_cparams(2),
        name=f"band_attn_{qc}",
    )(q, k, v, bias, *extra)


def _bias_table(rel_bias, qc, w):
    lo = LEFT_CHUNKS * CHUNK + (qc - 1) - REL_CLIP
    hi = (w + qc - 1) - lo - (2 * REL_CLIP + 1)
    ext = jnp.pad(rel_bias.astype(F32), ((0, 0), (lo, max(hi, 0))), mode="edge")
    rows = [ext[:, qc - 1 - i:qc - 1 - i + w] for i in range(qc)]
    return jnp.stack(rows, axis=1).reshape(N_HEADS * qc, w)


def _mix_kernel(x_ref, y_ref, att_ref, gate_ref, glu_ref, wo_ref, wout_ref, g2_ref, wq_ref,
                x1_ref, xnt_ref, qp_ref):
    glu = jnp.dot(y_ref[...], glu_ref[...], preferred_element_type=F32)
    a = glu[:, :D_MODEL] * jax.nn.sigmoid(glu[:, D_MODEL:])
    b = jnp.dot(att_ref[...], wo_ref[...], preferred_element_type=F32)
    mixed = gate_ref[:, :D_MODEL].astype(F32) * a + gate_ref[:, D_MODEL:].astype(F32) * b
    x1 = x_ref[...] + jnp.dot(mixed.astype(BF16), wout_ref[...], preferred_element_type=F32)
    x1_ref[...] = x1
    ms = jnp.mean(x1 * x1, axis=-1, keepdims=True)
    xn = x1 * lax.rsqrt(ms + RMS_EPS) * g2_ref[...]
    xnt_ref[...] = xn.T.astype(BF16)
    qp_ref[...] = jnp.dot(xn.astype(BF16), wq_ref[...], preferred_element_type=F32).astype(BF16)


def _mix(x, y, att, gates, glu_w, w_o, w_out, g2, w_q):
    t = x.shape[0]
    tm = TOKEN_TILE
    row = lambda n: pl.BlockSpec((tm, n), lambda i: (i, 0))
    dq = PEER_HEADS * PEER_DK
    return pl.pallas_call(
        _mix_kernel,
        grid=(t // tm,),
        in_specs=[row(D_MODEL), row(D_SSM), row(D_ATTN), row(2 * D_MODEL),
                  _const_spec((D_SSM, 2 * D_MODEL)), _const_spec((D_ATTN, D_MODEL)),
                  _const_spec((D_MODEL, D_MODEL)), _const_spec((1, D_MODEL)), _const_spec((D_MODEL, dq))],
        out_specs=[row(D_MODEL), pl.BlockSpec((D_MODEL, tm), lambda i: (0, i)), row(dq)],
        out_shape=[jax.ShapeDtypeStruct((t, D_MODEL), F32), jax.ShapeDtypeStruct((D_MODEL, t), BF16),
                   jax.ShapeDtypeStruct((t, dq), BF16)],
        compiler_params=_cparams(1),
        name="mix",
    )(x, y, att, gates, glu_w, w_o, w_out, g2, w_q)


def _top_sorted(quarters, quarter_ids, quarter_tags=None):
    v = list(quarters)
    cols = [list(quarter_ids)] + ([list(quarter_tags)] if quarter_tags is not None else [])
    ids = cols[0]

    def order(a, b, ids_ordered):
        first = v[a] >= v[b] if ids_ordered else (v[a] > v[b]) | ((v[a] == v[b]) & (ids[a] < ids[b]))
        v[a], v[b] = jnp.where(first, v[a], v[b]), jnp.where(first, v[b], v[a])
        for c in cols:
            c[a], c[b] = jnp.where(first, c[a], c[b]), jnp.where(first, c[b], c[a])

    order(0, 1, True)
    order(2, 3, True)
    order(0, 2, True)
    order(1, 3, True)
    order(1, 2, False)
    id_bound = float(2 ** 24)
    tops = [[] for _ in range(1 + len(cols))]
    for _ in range(PEER_TOPK):
        m = jnp.max(v[0], axis=0, keepdims=True)
        key = jnp.min(jnp.where(v[0] == m, ids[0], id_bound), axis=0, keepdims=True)
        won = ids[0] == key
        tops[0].append(m)
        tops[1].append(key)
        if quarter_tags is not None:
            tops[2].append(jnp.max(jnp.where(won, cols[1][0], -1.0), axis=0, keepdims=True))
        for level in range(3):
            v[level] = jnp.where(won, v[level + 1], v[level])
            for c in cols:
                c[level] = jnp.where(won, c[level + 1], c[level])
        v[3] = jnp.where(won, NEG_INF, v[3])
    return tops


def _route_kernel(qp_ref, keys_ref, rows_ref, g_ref, e_ref):
    tm = ROUTE_TILE
    nk, nc = N_KEYS // 4, N_CAND_ROWS // 4
    quarter = lax.broadcasted_iota(jnp.int32, (nk, tm), 0).astype(F32)
    key_ids = [quarter + float(i * nk) for i in range(4)]
    cand_ids = [quarter[:nc] + float(i * nc) for i in range(4)]
    sub_masks = [lax.broadcasted_iota(jnp.int32, (8, tm), 0) == r for r in range(8)]

    def stack_rows(rows, n_rows, fill):
        groups = []
        for first in range(0, n_rows, 8):
            acc = jnp.full((8, tm), fill, F32)
            for r, value in enumerate(rows[first:first + 8]):
                acc = jnp.where(sub_masks[r], value, acc)
            groups.append(acc)
        return jnp.concatenate(groups, axis=0)

    def head(h, _):
        vals, ids = [], []
        for z in range(2):
            hz = 2 * h + z
            q = qp_ref[:, pl.ds(pl.multiple_of(hz * PEER_HALF, PEER_HALF), PEER_HALF)]
            s = lax.dot_general(keys_ref[hz], q, (((1,), (1,)), ((), ())),
                                preferred_element_type=F32)
            v_z, i_z = _top_sorted([s[i * nk:(i + 1) * nk] for i in range(4)], key_ids)
            vals.append(v_z)
            ids.append(i_z)
        sums = [vals[0][i] + vals[1][j] for i, j in PEER_CAND]
        experts = [ids[0][i] * float(N_KEYS) + ids[1][j] for i, j in PEER_CAND]
        best_s, _, best_e = _top_sorted(
            [stack_rows(sums[i * nc:(i + 1) * nc], nc, NEG_INF) for i in range(4)], cand_ids,
            [stack_rows(experts[i * nc:(i + 1) * nc], nc, 0.0) for i in range(4)])
        best = stack_rows([jnp.exp(m - best_s[0]) for m in best_s], PEER_TOPK, 0.0)
        out_rows = pl.ds(pl.multiple_of(h * PEER_TOPK, PEER_TOPK), PEER_TOPK)
        e_ref[out_rows, :] = stack_rows(best_e, PEER_TOPK, 0.0)
        g_ref[out_rows, :] = best / jnp.sum(best, axis=0, keepdims=True)
        return 0

    def heads(i, _):
        for u in range(ROUTE_HEADS_PER_STEP):
            head(i * ROUTE_HEADS_PER_STEP + u, 0)
        return 0

    lax.fori_loop(0, PEER_HEADS // ROUTE_HEADS_PER_STEP, heads, 0)
    rows_ref[...] = e_ref[...].T.astype(jnp.int32) * ROW_WORDS


def _route(qp, keys_bf):
    t = qp.shape[0]
    tm = ROUTE_TILE
    out = pl.BlockSpec((PEER_SLOTS, tm), lambda i: (0, i))
    return pl.pallas_call(
        _route_kernel,
        grid=(t // tm,),
        in_specs=[pl.BlockSpec((tm, PEER_HEADS * PEER_DK), lambda i: (i, 0)),
                  _const_spec((2 * PEER_HEADS, N_KEYS, PEER_HALF))],
        out_specs=[pl.BlockSpec((tm, PEER_SLOTS), lambda i: (i, 0)), out],
        out_shape=[jax.ShapeDtypeStruct((t, PEER_SLOTS), jnp.int32),
                   jax.ShapeDtypeStruct((PEER_SLOTS, t), F32)],
        scratch_shapes=[pltpu.VMEM((PEER_SLOTS, tm), F32)],
        compiler_params=_cparams(1),
        name="peer_route",
    )(qp, keys_bf)


def _pack_table(tab):
    tb = tab.astype(BF16)
    hi = lax.bitcast_convert_type(tb[:, :512], jnp.uint16).astype(jnp.uint32)
    lo = lax.bitcast_convert_type(tb[:, 512:], jnp.uint16).astype(jnp.uint32)
    return ((hi << 16) | lo).reshape(N_EXPERTS * ROW_WORDS, 128)


def _gather_group(idx_ref, g, tab_ref, bufs):
    for u, buf in enumerate(bufs):
        for a in range(PEER_SLOTS // IDX_QUAD):
            quad = idx_ref.at[g * len(bufs) + u, pl.ds(a * IDX_QUAD, IDX_QUAD)]
            for b in range(IDX_QUAD):
                k = a * IDX_QUAD + b
                row = pl.multiple_of(quad[b], ROW_WORDS)
                buf[pl.ds(k, ROW_WORDS, stride=G_STRIDE), :] = tab_ref[pl.ds(row, ROW_WORDS), :]


def _lane_tile(buf_ref, j):
    word = buf_ref[j * G_STRIDE:j * G_STRIDE + PEER_SLOTS, :]
    hi = pltpu.bitcast(word & jnp.uint32(0xFFFF0000), F32)
    lo = pltpu.bitcast(word << 16, F32)
    return hi, lo


def _token_pipeline(idx_ref, next_idx_ref, tab_ref, bufs, compute_group, split_regions):
    group = len(bufs) // 2
    n_groups = PEER_TOKENS // group
    halves = (bufs[:group], bufs[group:])
    step = pl.program_id(0)

    @pl.when(step == 0)
    def _():
        _gather_group(idx_ref, 0, tab_ref, halves[0])

    def pair(g, half, src_ref, src_group):
        def run():
            _gather_group(src_ref, src_group, tab_ref, halves[1 - half])
            compute_group(g * group, halves[half])

        if split_regions:
            pl.when(step >= 0)(run)
        else:
            run()

    def body(i, _):
        for half in range(2):
            pair(2 * i + half, half, idx_ref, 2 * i + half + 1)
        return 0

    lax.fori_loop(0, n_groups // 2 - 1, body, 0)
    pair(n_groups - 2, 0, idx_ref, n_groups - 1)
    pair(n_groups - 1, 1, next_idx_ref, 0)


def _gathered_rows(buf):
    tiles = [_lane_tile(buf, j) for j in range(ROW_WORDS)]
    return jnp.concatenate([tl[0] for tl in tiles] + [tl[1] for tl in tiles], axis=1)


def _peer_u_kernel(idx_ref, next_idx_ref, tab_ref, xt_ref, g_ref, w_ref, xw_ref, act_ref, *bufs):
    lane = lax.broadcasted_iota(jnp.int32, (PEER_SLOTS, PEER_TOKENS), 1)
    xw_ref[...] = xt_ref[...].astype(F32)

    def compute_group(t0, bufs):
        for first in range(0, len(bufs), U_DOT_TOKENS):
            part = bufs[first:first + U_DOT_TOKENS]
            lhs = jnp.concatenate([_gathered_rows(b) for b in part], axis=0)
            r = jnp.dot(lhs, xw_ref[...], preferred_element_type=F32)
            for u in range(U_DOT_TOKENS):
                pltpu.store(act_ref, r[u * PEER_SLOTS:(u + 1) * PEER_SLOTS], mask=lane == t0 + first + u)

    _token_pipeline(idx_ref, next_idx_ref, tab_ref, bufs, compute_group, split_regions=True)
    act = act_ref[...]
    gelu = 0.5 * act * (1.0 + lax.erf(act * (1.0 / math.sqrt(2.0))))
    w_ref[...] = g_ref[...] * gelu


def _peer_v_kernel(idx_ref, next_idx_ref, tab_ref, wgt_ref, x_ref, o_ref, wt_ref, *bufs):
    sub = lax.broadcasted_iota(jnp.int32, (8, PEER_SLOTS), 0)
    wt_ref[...] = wgt_ref[...].T

    def compute_group(t0, bufs):
        for u, buf in enumerate(bufs):
            w = wt_ref[pl.ds(t0 + u, 1), :]
            w0 = w.astype(BF16).astype(F32)
            r1 = w - w0
            w1 = r1.astype(BF16).astype(F32)
            w2 = r1 - w1
            lhs = jnp.where(sub == 0, w0, jnp.where(sub == 1, w1, jnp.where(sub == 2, w2, 0.0)))
            r = jnp.dot(lhs, _gathered_rows(buf), preferred_element_type=F32)
            o_ref[pl.ds(t0 + u, 1), :] = x_ref[pl.ds(t0 + u, 1), :] + (r[0:1] + r[1:2] + r[2:3])

    _token_pipeline(idx_ref, next_idx_ref, tab_ref, bufs, compute_group, split_regions=False)


def _idx_specs(n_steps):
    shape = (PEER_TOKENS, PEER_SLOTS)
    return [pl.BlockSpec(shape, lambda i: (i, 0), memory_space=pltpu.SMEM),
            pl.BlockSpec(shape, lambda i: (jnp.minimum(i + 1, n_steps - 1), 0), memory_space=pltpu.SMEM)]


def _table_spec():
    return pl.BlockSpec((N_EXPERTS * ROW_WORDS, 128), lambda i: (0, 0), pipeline_mode=pl.Buffered(1))


def _gather_buffers(group):
    return [pltpu.VMEM((G_ROWS, 128), jnp.uint32) for _ in range(2 * group)]


def _peer_u(rows, tab, xt, g_t):
    t = xt.shape[1]
    n_steps = t // PEER_TOKENS
    col = pl.BlockSpec((PEER_SLOTS, PEER_TOKENS), lambda i: (0, i))
    return pl.pallas_call(
        _peer_u_kernel,
        grid=(n_steps,),
        in_specs=_idx_specs(n_steps) + [_table_spec(),
                                        pl.BlockSpec((D_MODEL, PEER_TOKENS), lambda i: (0, i)), col],
        out_specs=col,
        out_shape=jax.ShapeDtypeStruct((PEER_SLOTS, t), F32),
        scratch_shapes=[pltpu.VMEM((D_MODEL, PEER_TOKENS), F32),
                        pltpu.VMEM((PEER_SLOTS, PEER_TOKENS), F32)] + _gather_buffers(PEER_U_GROUP),
        compiler_params=_cparams(1),
        name="peer_u",
    )(rows, rows, tab, xt, g_t)


def _peer_v(rows, tab, wgt_t, x):
    t = x.shape[0]
    n_steps = t // PEER_TOKENS
    row = pl.BlockSpec((PEER_TOKENS, D_MODEL), lambda i: (i, 0))
    col = pl.BlockSpec((PEER_SLOTS, PEER_TOKENS), lambda i: (0, i))
    return pl.pallas_call(
        _peer_v_kernel,
        grid=(n_steps,),
        in_specs=_idx_specs(n_steps) + [_table_spec(), col, row],
        out_specs=row,
        out_shape=jax.ShapeDtypeStruct((t, D_MODEL), F32),
        scratch_shapes=[pltpu.VMEM((PEER_TOKENS, PEER_SLOTS), F32)] + _gather_buffers(PEER_V_GROUP),
        compiler_params=_cparams(1),
        name="peer_v",
    )(rows, rows, tab, wgt_t, x)


def _final_norm_kernel(x_ref, g_ref, y_ref):
    x = x_ref[...]
    ms = jnp.mean(x * x, axis=-1, keepdims=True)
    y_ref[...] = x * lax.rsqrt(ms + RMS_EPS) * g_ref[...]


def _final_norm(x, g, first_row, n_rows):
    tm = TOKEN_TILE
    base = first_row // tm
    return pl.pallas_call(
        _final_norm_kernel,
        grid=(n_rows // tm,),
        in_specs=[pl.BlockSpec((tm, D_MODEL), lambda i: (base + i, 0)), _const_spec((1, D_MODEL))],
        out_specs=pl.BlockSpec((tm, D_MODEL), lambda i: (i, 0)),
        out_shape=jax.ShapeDtypeStruct((n_rows, D_MODEL), F32),
        compiler_params=_cparams(1),
        name="final_norm",
    )(x, g)


def kernel(x_prompt, x_sample, cache_k, cache_v, state_ssm_re, state_ssm_im, norm1_g, w_in, ssm_a_re, ssm_a_im, ssm_log_dt, ssm_b_re, ssm_b_im, ssm_c_re, ssm_c_im, ssm_d, ssm_glu_w, attn_rel_bias, attn_w_o, w_out, norm2_g, peer_w_q, peer_sub_keys, peer_u, peer_v, final_g):
    bp, lp, _ = x_prompt.shape
    bs, ls, _ = x_sample.shape
    depth = w_in.shape[0]
    tp = bp * lp
    kv_win = cache_k.shape[2]
    keep = min(LEFT_CHUNKS * CHUNK, lp)
    x = jnp.concatenate([x_prompt.reshape(tp, D_MODEL), x_sample.reshape(bs * ls, D_MODEL)], axis=0)
    t = x.shape[0]
    assert t % TOKEN_TILE == 0 and lp % SCAN_CHUNK == 0 and lp % CHUNK == 0 and kv_win == LEFT_CHUNKS * CHUNK

    zeros_state = jnp.zeros((bp, N_SSM), F32)
    outs = {n: [] for n in ("p_re", "p_im", "p_k", "p_v", "s_re", "s_im", "s_k", "s_v")}
    for l in range(depth):
        u, q, kv, kvb, gates = _inproj(x, norm1_g[l][None], w_in[l].astype(BF16))

        ssm_args = (ssm_a_re[l], ssm_a_im[l], ssm_log_dt[l], ssm_b_re[l], ssm_b_im[l], ssm_c_re[l], ssm_c_im[l])
        dskip = ssm_d[l][None]
        bcat, apr, api, ccat = _ssm_params(*ssm_args, SCAN_CHUNK)
        y, hr_p, hi_p = _ssm(u, None, bp, lp, 0, zeros_state, zeros_state,
                             bcat, apr, api, ccat, dskip, SCAN_CHUNK)
        y, hr_s, hi_s = _ssm(u, y, bs, ls, tp, state_ssm_re[l].reshape(bs, N_SSM),
                             state_ssm_im[l].reshape(bs, N_SSM), bcat, apr[:ls], api[:ls], ccat, dskip, ls)

        pad = ((0, 0), (LEFT_CHUNKS * CHUNK, 0), (0, 0))
        kb_p = jnp.pad(kvb[:tp, :D_ATTN].reshape(bp, lp, D_ATTN), pad)
        vb_p = jnp.pad(kvb[:tp, D_ATTN:].reshape(bp, lp, D_ATTN), pad)
        w_p = (LEFT_CHUNKS + 1) * CHUNK
        att = _attention(q, None, bp, lp, 0, kb_p, vb_p,
                         _bias_table(attn_rel_bias[l], CHUNK, w_p), CHUNK, w_p, True)
        kb_s = jnp.concatenate([cache_k[l].reshape(bs, kv_win, D_ATTN).astype(BF16),
                                kvb[tp:, :D_ATTN].reshape(bs, ls, D_ATTN)], axis=1)
        vb_s = jnp.concatenate([cache_v[l].reshape(bs, kv_win, D_ATTN).astype(BF16),
                                kvb[tp:, D_ATTN:].reshape(bs, ls, D_ATTN)], axis=1)
        w_s = kv_win + ls
        att = _attention(q, att, bs, ls, tp, kb_s, vb_s,
                         _bias_table(attn_rel_bias[l], ls, w_s), ls, w_s, False)

        x1, xn_t, qp = _mix(x, y, att, gates, ssm_glu_w[l].astype(BF16), attn_w_o[l].astype(BF16),
                          w_out[l].astype(BF16), norm2_g[l][None], peer_w_q[l].astype(BF16))
        keys = peer_sub_keys[l].reshape(2 * PEER_HEADS, N_KEYS, PEER_HALF).astype(BF16)
        rows, g_t = _route(qp, keys)
        wgt_t = _peer_u(rows, _pack_table(peer_u[l]), xn_t, g_t)
        x = _peer_v(rows, _pack_table(peer_v[l]), wgt_t, x1)

        kv_p = jnp.stack([kv[(i + 1) * lp - keep:(i + 1) * lp] for i in range(bp)])
        kv_p = kv_p.reshape(bp, keep, 2, N_HEADS, HEAD_DIM)
        kv_s = kv[tp:].reshape(bs, ls, 2, N_HEADS, HEAD_DIM)
        outs["p_re"].append(hr_p.reshape(bp, N_GROUPS, SSM_STATE))
        outs["p_im"].append(hi_p.reshape(bp, N_GROUPS, SSM_STATE))
        outs["p_k"].append(kv_p[:, :, 0])
        outs["p_v"].append(kv_p[:, :, 1])
        outs["s_re"].append(hr_s.reshape(bs, N_GROUPS, SSM_STATE))
        outs["s_im"].append(hi_s.reshape(bs, N_GROUPS, SSM_STATE))
        outs["s_k"].append(kv_s[:, :, 0])
        outs["s_v"].append(kv_s[:, :, 1])

    y_p = _final_norm(x, final_g[None], 0, tp)
    y_s = _final_norm(x, final_g[None], tp, bs * ls)
    st = {n: jnp.stack(v) for n, v in outs.items()}
    return (y_p.reshape(bp, lp, D_MODEL), y_s.reshape(bs, ls, D_MODEL),
            st["p_re"], st["p_im"], st["p_k"], st["p_v"], st["s_re"], st["s_im"], st["s_k"], st["s_v"])
```

```python
import functools
import math

import jax
import jax.numpy as jnp
from jax import lax
from jax.experimental import pallas as pl
from jax.experimental.pallas import tpu as pltpu

F32 = jnp.float32
BF16 = jnp.bfloat16

D_MODEL = 1024
CHUNK = 64
D_SSM = 512
SSM_GROUP = 16
N_GROUPS = D_SSM // SSM_GROUP
SSM_STATE = 64
N_SSM = N_GROUPS * SSM_STATE
N_HEADS = 8
HEAD_DIM = 64
D_ATTN = N_HEADS * HEAD_DIM
LEFT_CHUNKS = 8
REL_CLIP = 128
ATTN_SCALE = HEAD_DIM ** -0.5
N_KEYS = 128
N_EXPERTS = N_KEYS * N_KEYS
PEER_HEADS = 8
PEER_DK = 256
PEER_HALF = PEER_DK // 2
PEER_TOPK = 16
PEER_SLOTS = PEER_HEADS * PEER_TOPK
RMS_EPS = 1e-6
D_IN = D_SSM + 3 * D_ATTN + 2 * D_MODEL

VMEM_LIMIT_BYTES = 56 * 1024 * 1024
TOKEN_TILE = 256
ROUTE_TILE = 128
ROUTE_HEADS_PER_STEP = 8
PEER_TOKENS = 128
PEER_U_GROUP = 8
PEER_V_GROUP = 16
U_DOT_TOKENS = 4
ATTN_CHUNKS_PER_STEP = 4
SCAN_CHUNK = 256
SCAN_ROWS = 8
NEG_INF = float("-inf")

ROW_WORDS = 4
G_STRIDE = 136
G_ROWS = ROW_WORDS * G_STRIDE
IDX_QUAD = 8

PEER_CAND = [(i, j) for i in range(PEER_TOPK) for j in range(PEER_TOPK) if (i + 1) * (j + 1) <= PEER_TOPK]
N_CAND_ROWS = 64


def _cparams(n_axes):
    return pltpu.CompilerParams(dimension_semantics=("arbitrary",) * n_axes,
                                vmem_limit_bytes=VMEM_LIMIT_BYTES)


def _const_spec(shape):
    return pl.BlockSpec(shape, lambda *_: (0,) * len(shape))


def _inproj_kernel(x_ref, g_ref, w_ref, u_ref, q_ref, kv_ref, kvb_ref, gate_ref):
    x = x_ref[...]
    ms = jnp.mean(x * x, axis=-1, keepdims=True)
    xn = (x * lax.rsqrt(ms + RMS_EPS) * g_ref[...]).astype(BF16)

    def proj(lo, hi):
        return jnp.dot(xn, w_ref[:, lo:hi], preferred_element_type=F32)

    u_ref[...] = proj(0, D_SSM).astype(BF16)
    q_ref[...] = proj(D_SSM, D_SSM + D_ATTN).astype(BF16)
    for c in range(2):
        lo = D_SSM + D_ATTN + c * D_ATTN
        kv = proj(lo, lo + D_ATTN)
        kv_ref[:, c * D_ATTN:(c + 1) * D_ATTN] = kv
        kvb_ref[:, c * D_ATTN:(c + 1) * D_ATTN] = kv.astype(BF16)
    for c in range(4):
        lo = D_SSM + 3 * D_ATTN + c * 512
        gate_ref[:, c * 512:(c + 1) * 512] = jax.nn.sigmoid(proj(lo, lo + 512)).astype(BF16)


def _inproj(x, g, w_bf):
    t = x.shape[0]
    tm = TOKEN_TILE
    row = lambda n: pl.BlockSpec((tm, n), lambda i: (i, 0))
    return pl.pallas_call(
        _inproj_kernel,
        grid=(t // tm,),
        in_specs=[row(D_MODEL), _const_spec((1, D_MODEL)), _const_spec((D_MODEL, D_IN))],
        out_specs=[row(D_SSM), row(D_ATTN), row(2 * D_ATTN), row(2 * D_ATTN), row(2 * D_MODEL)],
        out_shape=[jax.ShapeDtypeStruct((t, D_SSM), BF16), jax.ShapeDtypeStruct((t, D_ATTN), BF16),
                   jax.ShapeDtypeStruct((t, 2 * D_ATTN), F32), jax.ShapeDtypeStruct((t, 2 * D_ATTN), BF16),
                   jax.ShapeDtypeStruct((t, 2 * D_MODEL), BF16)],
        compiler_params=_cparams(1),
        name="inproj",
    )(x, g, w_bf)


def _ssm_kernel(lc, u_ref, bcat_ref, apr_ref, api_ref, ccat_ref, d_ref, h0r_ref, h0i_ref, *rest):
    y_ref, hr_ref, hi_ref, re_ref, im_ref, cr_ref, ci_ref = rest[-7:]
    c = pl.program_id(1)

    @pl.when(c == 0)
    def _():
        cr_ref[...] = h0r_ref[0]
        ci_ref[...] = h0i_ref[0]

    u = u_ref[...]
    hc, hs = D_SSM // 2, N_SSM // 2
    for half in range(2):
        uh = u[:, half * hc:(half + 1) * hc]
        st = slice(half * hs, (half + 1) * hs)
        re_ref[:, st] = jnp.dot(uh, bcat_ref[half * hc:(half + 1) * hc, st], preferred_element_type=F32)
        im_ref[:, st] = jnp.dot(uh, bcat_ref[half * hc:(half + 1) * hc, N_SSM + half * hs:N_SSM + (half + 1) * hs],
                                preferred_element_type=F32)

    row = lax.broadcasted_iota(jnp.int32, (SCAN_ROWS, 128), 0)

    def lane_tile(j, _):
        sl = pl.ds(pl.multiple_of(j * 128, 128), 128)
        steps = []
        s = 1
        while s < SCAN_ROWS:
            steps.append((s, apr_ref[s - 1:s, sl], api_ref[s - 1:s, sl], row >= s))
            s *= 2
        pr = apr_ref[0:SCAN_ROWS, sl]
        pi = api_ref[0:SCAN_ROWS, sl]
        c_r = cr_ref[:, sl]
        c_i = ci_ref[:, sl]
        for blk in range(lc // SCAN_ROWS):
            rows = slice(blk * SCAN_ROWS, (blk + 1) * SCAN_ROWS)
            hr = re_ref[rows, sl]
            hi = im_ref[rows, sl]
            for s, ar, ai, keep in steps:
                sr = jnp.where(keep, pltpu.roll(hr, s, 0), 0.0)
                si = jnp.where(keep, pltpu.roll(hi, s, 0), 0.0)
                hr, hi = hr + (ar * sr - ai * si), hi + (ar * si + ai * sr)
            hr, hi = hr + (pr * c_r - pi * c_i), hi + (pr * c_i + pi * c_r)
            re_ref[rows, sl] = hr
            im_ref[rows, sl] = hi
            c_r = hr[SCAN_ROWS - 1:SCAN_ROWS]
            c_i = hi[SCAN_ROWS - 1:SCAN_ROWS]
        cr_ref[:, sl] = c_r
        ci_ref[:, sl] = c_i
        return 0

    lax.fori_loop(0, N_SSM // 128, lane_tile, 0)

    hr_ref[0] = cr_ref[...]
    hi_ref[0] = ci_ref[...]
    for half in range(2):
        st = slice(half * hs, (half + 1) * hs)
        ch = slice(half * hc, (half + 1) * hc)
        y = jnp.dot(re_ref[:, st].astype(BF16), ccat_ref[half * hs:(half + 1) * hs, ch],
                    preferred_element_type=F32)
        y = y + jnp.dot(im_ref[:, st].astype(BF16), ccat_ref[N_SSM + half * hs:N_SSM + (half + 1) * hs, ch],
                        preferred_element_type=F32)
        y_ref[:, ch] = (y + d_ref[:, ch] * u[:, ch].astype(F32)).astype(BF16)


def _seq_rows_spec(rows, width, first_row, seq_len):
    per_seq = seq_len // rows
    base = first_row // rows
    return pl.BlockSpec((rows, width), lambda i, n: (base + i * per_seq + n, 0))


def _alias_prev(prev, n_inputs):
    if prev is None:
        return [], [], {}
    return [prev], [pl.BlockSpec(memory_space=pl.ANY)], {n_inputs: 0}


def _ssm(u, prev_y, b, l, first_row, h0r, h0i, bcat, apr, api, ccat, dskip, lc):
    st = jax.ShapeDtypeStruct((b, 1, N_SSM), F32)
    state_spec = pl.BlockSpec((1, 1, N_SSM), lambda i, c: (i, 0, 0))
    rows = _seq_rows_spec(lc, D_SSM, first_row, l)
    extra, extra_specs, aliases = _alias_prev(prev_y, 8)
    return pl.pallas_call(
        functools.partial(_ssm_kernel, lc),
        grid=(b, l // lc),
        in_specs=[rows, _const_spec((D_SSM, 2 * N_SSM)), _const_spec((lc, N_SSM)), _const_spec((lc, N_SSM)),
                  _const_spec((2 * N_SSM, D_SSM)), _const_spec((1, D_SSM)), state_spec, state_spec] + extra_specs,
        out_specs=[rows, state_spec, state_spec],
        out_shape=[jax.ShapeDtypeStruct(u.shape, BF16), st, st],
        scratch_shapes=[pltpu.VMEM((lc, N_SSM), F32), pltpu.VMEM((lc, N_SSM), F32),
                        pltpu.VMEM((1, N_SSM), F32), pltpu.VMEM((1, N_SSM), F32)],
        input_output_aliases=aliases,
        compiler_params=_cparams(2),
        name=f"ssm_scan_{lc}",
    )(u, bcat, apr, api, ccat, dskip, h0r.reshape(b, 1, N_SSM), h0i.reshape(b, 1, N_SSM), *extra)


def _ssm_params(a_re, a_im, log_dt, b_re, b_im, c_re, c_im, lc):
    a = lax.complex(a_re, a_im)
    adt = a * jnp.exp(log_dt)[:, None]
    a_bar = jnp.exp(adt)
    b_bar = ((a_bar - 1.0) / a)[..., None] * lax.complex(b_re, b_im)
    eye = jnp.eye(N_GROUPS, dtype=F32)
    bre = jnp.einsum("gpc,gh->gchp", jnp.real(b_bar), eye).reshape(D_SSM, N_SSM)
    bim = jnp.einsum("gpc,gh->gchp", jnp.imag(b_bar), eye).reshape(D_SSM, N_SSM)
    bcat = jnp.concatenate([bre, bim], axis=1).astype(BF16)
    cre = jnp.einsum("gcp,gh->gphc", c_re, eye).reshape(N_SSM, D_SSM)
    cim = jnp.einsum("gcp,gh->gphc", c_im, eye).reshape(N_SSM, D_SSM)
    ccat = jnp.concatenate([cre, -cim], axis=0).astype(BF16)
    steps = jnp.arange(1, lc + 1, dtype=F32)[:, None, None]
    apow = jnp.exp(adt[None] * steps).reshape(lc, N_SSM)
    return bcat, jnp.real(apow), jnp.imag(apow), ccat


def _attn_kernel(qc, per_step, w, masked, q_ref, k_ref, v_ref, bias_ref, *rest):
    o_ref = rest[-1]
    second = lax.broadcasted_iota(jnp.int32, (qc, 2 * HEAD_DIM), 1) >= HEAD_DIM
    for c in range(per_step):
        n = pl.program_id(1) * per_step + c
        qrows = slice(c * qc, (c + 1) * qc)
        start = pl.multiple_of(n * qc, qc)
        scores = []
        for pr in range(N_HEADS // 2):
            lanes = slice(pr * 2 * HEAD_DIM, (pr + 1) * 2 * HEAD_DIM)
            q2 = q_ref[qrows, lanes]
            zero = jnp.zeros_like(q2)
            qs = jnp.concatenate([jnp.where(second, zero, q2), jnp.where(second, q2, zero)], axis=0)
            k2 = k_ref[0, pl.ds(start, w), lanes]
            scores.append(lax.dot_general(qs, k2, (((1,), (1,)), ((), ())), preferred_element_type=F32))
        s = jnp.concatenate(scores, axis=0)
        s = s * ATTN_SCALE + bias_ref[...]
        if masked:
            col = lax.broadcasted_iota(jnp.int32, (N_HEADS * qc, w), 1)
            s = jnp.where(col + n * qc >= LEFT_CHUNKS * CHUNK, s, jnp.finfo(F32).min)
        m = jnp.max(s, axis=-1, keepdims=True)
        p = jnp.exp(s - m)
        p = p / jnp.sum(p, axis=-1, keepdims=True)
        p = p.astype(BF16)
        for pr in range(N_HEADS // 2):
            lanes = slice(pr * 2 * HEAD_DIM, (pr + 1) * 2 * HEAD_DIM)
            v2 = v_ref[0, pl.ds(start, w), lanes]
            r = jnp.dot(p[2 * pr * qc:(2 * pr + 2) * qc], v2, preferred_element_type=F32)
            o_ref[qrows, lanes] = jnp.where(second, r[qc:], r[:qc]).astype(BF16)


def _attention(q, prev_att, b, l, first_row, k, v, bias, qc, per_step, w, masked):
    lk = k.shape[1]
    kv_spec = pl.BlockSpec((1, lk, D_ATTN), lambda i, n: (i, 0, 0))
    rows = _seq_rows_spec(qc * per_step, D_ATTN, first_row, l)
    extra, extra_specs, aliases = _alias_prev(prev_att, 4)
    return pl.pallas_call(
        functools.partial(_attn_kernel, qc, per_step, w, masked),
        grid=(b, l // (qc * per_step)),
        in_specs=[rows, kv_spec, kv_spec, _const_spec((N_HEADS * qc, w))] + extra_specs,
        out_specs=rows,
        out_shape=jax.ShapeDtypeStruct(q.shape, BF16),
        input_output_aliases=aliases,
        compiler_params=_cparams(2),
        name=f"band_attn_{qc}",
    )(q, k, v, bias, *extra)


def _bias_table(rel_bias, qc, w):
    lo = LEFT_CHUNKS * CHUNK + (qc - 1) - REL_CLIP
    hi = (w + qc - 1) - lo - (2 * REL_CLIP + 1)
    ext = jnp.pad(rel_bias.astype(F32), ((0, 0), (lo, max(hi, 0))), mode="edge")
    rows = [ext[:, qc - 1 - i:qc - 1 - i + w] for i in range(qc)]
    return jnp.stack(rows, axis=1).reshape(N_HEADS * qc, w)


def _mix_kernel(x_ref, y_ref, att_ref, gate_ref, glu_ref, wo_ref, wout_ref, g2_ref, wq_ref,
                x1_ref, xnt_ref, qp_ref):
    glu = jnp.dot(y_ref[...], glu_ref[...], preferred_element_type=F32)
    a = glu[:, :D_MODEL] * jax.nn.sigmoid(glu[:, D_MODEL:])
    b = jnp.dot(att_ref[...], wo_ref[...], preferred_element_type=F32)
    mixed = gate_ref[:, :D_MODEL].astype(F32) * a + gate_ref[:, D_MODEL:].astype(F32) * b
    x1 = x_ref[...] + jnp.dot(mixed.astype(BF16), wout_ref[...], preferred_element_type=F32)
    x1_ref[...] = x1
    ms = jnp.mean(x1 * x1, axis=-1, keepdims=True)
    xn = x1 * lax.rsqrt(ms + RMS_EPS) * g2_ref[...]
    xnt_ref[...] = xn.T.astype(BF16)
    qp_ref[...] = jnp.dot(xn.astype(BF16), wq_ref[...], preferred_element_type=F32).astype(BF16)


def _mix(x, y, att, gates, glu_w, w_o, w_out, g2, w_q):
    t = x.shape[0]
    tm = TOKEN_TILE
    row = lambda n: pl.BlockSpec((tm, n), lambda i: (i, 0))
    dq = PEER_HEADS * PEER_DK
    return pl.pallas_call(
        _mix_kernel,
        grid=(t // tm,),
        in_specs=[row(D_MODEL), row(D_SSM), row(D_ATTN), row(2 * D_MODEL),
                  _const_spec((D_SSM, 2 * D_MODEL)), _const_spec((D_ATTN, D_MODEL)),
                  _const_spec((D_MODEL, D_MODEL)), _const_spec((1, D_MODEL)), _const_spec((D_MODEL, dq))],
        out_specs=[row(D_MODEL), pl.BlockSpec((D_MODEL, tm), lambda i: (0, i)), row(dq)],
        out_shape=[jax.ShapeDtypeStruct((t, D_MODEL), F32), jax.ShapeDtypeStruct((D_MODEL, t), BF16),
                   jax.ShapeDtypeStruct((t, dq), BF16)],
        compiler_params=_cparams(1),
        name="mix",
    )(x, y, att, gates, glu_w, w_o, w_out, g2, w_q)


def _top_sorted(quarters, quarter_ids, quarter_tags=None):
    v = list(quarters)
    cols = [list(quarter_ids)] + ([list(quarter_tags)] if quarter_tags is not None else [])
    ids = cols[0]

    def order(a, b, ids_ordered):
        first = v[a] >= v[b] if ids_ordered else (v[a] > v[b]) | ((v[a] == v[b]) & (ids[a] < ids[b]))
        v[a], v[b] = jnp.where(first, v[a], v[b]), jnp.where(first, v[b], v[a])
        for c in cols:
            c[a], c[b] = jnp.where(first, c[a], c[b]), jnp.where(first, c[b], c[a])

    order(0, 1, True)
    order(2, 3, True)
    order(0, 2, True)
    order(1, 3, True)
    order(1, 2, False)
    id_bound = float(2 ** 24)
    tops = [[] for _ in range(1 + len(cols))]
    for _ in range(PEER_TOPK):
        m = jnp.max(v[0], axis=0, keepdims=True)
        key = jnp.min(jnp.where(v[0] == m, ids[0], id_bound), axis=0, keepdims=True)
        won = ids[0] == key
        tops[0].append(m)
        tops[1].append(key)
        if quarter_tags is not None:
            tops[2].append(jnp.max(jnp.where(won, cols[1][0], -1.0), axis=0, keepdims=True))
        for level in range(3):
            v[level] = jnp.where(won, v[level + 1], v[level])
            for c in cols:
                c[level] = jnp.where(won, c[level + 1], c[level])
        v[3] = jnp.where(won, NEG_INF, v[3])
    return tops


def _route_kernel(qp_ref, keys_ref, rows_ref, g_ref, e_ref):
    tm = ROUTE_TILE
    nk, nc = N_KEYS // 4, N_CAND_ROWS // 4
    quarter = lax.broadcasted_iota(jnp.int32, (nk, tm), 0).astype(F32)
    key_ids = [quarter + float(i * nk) for i in range(4)]
    cand_ids = [quarter[:nc] + float(i * nc) for i in range(4)]
    sub_masks = [lax.broadcasted_iota(jnp.int32, (8, tm), 0) == r for r in range(8)]

    def stack_rows(rows, n_rows, fill):
        groups = []
        for first in range(0, n_rows, 8):
            acc = jnp.full((8, tm), fill, F32)
            for r, value in enumerate(rows[first:first + 8]):
                acc = jnp.where(sub_masks[r], value, acc)
            groups.append(acc)
        return jnp.concatenate(groups, axis=0)

    def head(h, _):
        vals, ids = [], []
        for z in range(2):
            hz = 2 * h + z
            q = qp_ref[:, pl.ds(pl.multiple_of(hz * PEER_HALF, PEER_HALF), PEER_HALF)]
            s = lax.dot_general(keys_ref[hz], q, (((1,), (1,)), ((), ())),
                                preferred_element_type=F32)
            v_z, i_z = _top_sorted([s[i * nk:(i + 1) * nk] for i in range(4)], key_ids)
            vals.append(v_z)
            ids.append(i_z)
        sums = [vals[0][i] + vals[1][j] for i, j in PEER_CAND]
        experts = [ids[0][i] * float(N_KEYS) + ids[1][j] for i, j in PEER_CAND]
        best_s, _, best_e = _top_sorted(
            [stack_rows(sums[i * nc:(i + 1) * nc], nc, NEG_INF) for i in range(4)], cand_ids,
            [stack_rows(experts[i * nc:(i + 1) * nc], nc, 0.0) for i in range(4)])
        best = stack_rows([jnp.exp(m - best_s[0]) for m in best_s], PEER_TOPK, 0.0)
        out_rows = pl.ds(pl.multiple_of(h * PEER_TOPK, PEER_TOPK), PEER_TOPK)
        e_ref[out_rows, :] = stack_rows(best_e, PEER_TOPK, 0.0)
        g_ref[out_rows, :] = best / jnp.sum(best, axis=0, keepdims=True)
        return 0

    def heads(i, _):
        for u in range(ROUTE_HEADS_PER_STEP):
            head(i * ROUTE_HEADS_PER_STEP + u, 0)
        return 0

    lax.fori_loop(0, PEER_HEADS // ROUTE_HEADS_PER_STEP, heads, 0)
    rows_ref[...] = e_ref[...].T.astype(jnp.int32) * ROW_WORDS


def _route(qp, keys_bf):
    t = qp.shape[0]
    tm = ROUTE_TILE
    out = pl.BlockSpec((PEER_SLOTS, tm), lambda i: (0, i))
    return pl.pallas_call(
        _route_kernel,
        grid=(t // tm,),
        in_specs=[pl.BlockSpec((tm, PEER_HEADS * PEER_DK), lambda i: (i, 0)),
                  _const_spec((2 * PEER_HEADS, N_KEYS, PEER_HALF))],
        out_specs=[pl.BlockSpec((tm, PEER_SLOTS), lambda i: (i, 0)), out],
        out_shape=[jax.ShapeDtypeStruct((t, PEER_SLOTS), jnp.int32),
                   jax.ShapeDtypeStruct((PEER_SLOTS, t), F32)],
        scratch_shapes=[pltpu.VMEM((PEER_SLOTS, tm), F32)],
        compiler_params=_cparams(1),
        name="peer_route",
    )(qp, keys_bf)


def _pack_table(tab):
    tb = tab.astype(BF16)
    hi = lax.bitcast_convert_type(tb[:, :512], jnp.uint16).astype(jnp.uint32)
    lo = lax.bitcast_convert_type(tb[:, 512:], jnp.uint16).astype(jnp.uint32)
    return ((hi << 16) | lo).reshape(N_EXPERTS * ROW_WORDS, 128)


def _gather_group(idx_ref, g, tab_ref, bufs):
    for u, buf in enumerate(bufs):
        for a in range(PEER_SLOTS // IDX_QUAD):
            quad = idx_ref.at[g * len(bufs) + u, pl.ds(a * IDX_QUAD, IDX_QUAD)]
            for b in range(IDX_QUAD):
                k = a * IDX_QUAD + b
                row = pl.multiple_of(quad[b], ROW_WORDS)
                buf[pl.ds(k, ROW_WORDS, stride=G_STRIDE), :] = tab_ref[pl.ds(row, ROW_WORDS), :]


def _lane_tile(buf_ref, j):
    word = buf_ref[j * G_STRIDE:j * G_STRIDE + PEER_SLOTS, :]
    hi = pltpu.bitcast(word & jnp.uint32(0xFFFF0000), F32)
    lo = pltpu.bitcast(word << 16, F32)
    return hi, lo


def _token_pipeline(idx_ref, next_idx_ref, tab_ref, bufs, compute_group, split_regions):
    group = len(bufs) // 2
    n_groups = PEER_TOKENS // group
    halves = (bufs[:group], bufs[group:])
    step = pl.program_id(0)

    @pl.when(step == 0)
    def _():
        _gather_group(idx_ref, 0, tab_ref, halves[0])

    def pair(g, half, src_ref, src_group):
        def run():
            _gather_group(src_ref, src_group, tab_ref, halves[1 - half])
            compute_group(g * group, halves[half])

        if split_regions:
            pl.when(step >= 0)(run)
        else:
            run()

    def body(i, _):
        for half in range(2):
            pair(2 * i + half, half, idx_ref, 2 * i + half + 1)
        return 0

    lax.fori_loop(0, n_groups // 2 - 1, body, 0)
    pair(n_groups - 2, 0, idx_ref, n_groups - 1)
    pair(n_groups - 1, 1, next_idx_ref, 0)


def _gathered_rows(buf):
    tiles = [_lane_tile(buf, j) for j in range(ROW_WORDS)]
    return jnp.concatenate([tl[0] for tl in tiles] + [tl[1] for tl in tiles], axis=1)


def _peer_u_kernel(idx_ref, next_idx_ref, tab_ref, xt_ref, g_ref, w_ref, xw_ref, act_ref, *bufs):
    lane = lax.broadcasted_iota(jnp.int32, (PEER_SLOTS, PEER_TOKENS), 1)
    xw_ref[...] = xt_ref[...].astype(F32)

    def compute_group(t0, bufs):
        for first in range(0, len(bufs), U_DOT_TOKENS):
            part = bufs[first:first + U_DOT_TOKENS]
            lhs = jnp.concatenate([_gathered_rows(b) for b in part], axis=0)
            r = jnp.dot(lhs, xw_ref[...], preferred_element_type=F32)
            for u in range(U_DOT_TOKENS):
                pltpu.store(act_ref, r[u * PEER_SLOTS:(u + 1) * PEER_SLOTS], mask=lane == t0 + first + u)

    _token_pipeline(idx_ref, next_idx_ref, tab_ref, bufs, compute_group, split_regions=True)
    act = act_ref[...]
    gelu = 0.5 * act * (1.0 + lax.erf(act * (1.0 / math.sqrt(2.0))))
    w_ref[...] = g_ref[...] * gelu


def _peer_v_kernel(idx_ref, next_idx_ref, tab_ref, wgt_ref, x_ref, o_ref, wt_ref, *bufs):
    sub = lax.broadcasted_iota(jnp.int32, (8, PEER_SLOTS), 0)
    wt_ref[...] = wgt_ref[...].T

    def compute_group(t0, bufs):
        for u, buf in enumerate(bufs):
            w = wt_ref[pl.ds(t0 + u, 1), :]
            w0 = w.astype(BF16).astype(F32)
            r1 = w - w0
            w1 = r1.astype(BF16).astype(F32)
            w2 = r1 - w1
            lhs = jnp.where(sub == 0, w0, jnp.where(sub == 1, w1, jnp.where(sub == 2, w2, 0.0)))
            r = jnp.dot(lhs, _gathered_rows(buf), preferred_element_type=F32)
            o_ref[pl.ds(t0 + u, 1), :] = x_ref[pl.ds(t0 + u, 1), :] + (r[0:1] + r[1:2] + r[2:3])

    _token_pipeline(idx_ref, next_idx_ref, tab_ref, bufs, compute_group, split_regions=False)


def _idx_specs(n_steps):
    shape = (PEER_TOKENS, PEER_SLOTS)
    return [pl.BlockSpec(shape, lambda i: (i, 0), memory_space=pltpu.SMEM),
            pl.BlockSpec(shape, lambda i: (jnp.minimum(i + 1, n_steps - 1), 0), memory_space=pltpu.SMEM)]


def _table_spec():
    return pl.BlockSpec((N_EXPERTS * ROW_WORDS, 128), lambda i: (0, 0), pipeline_mode=pl.Buffered(1))


def _gather_buffers(group):
    return [pltpu.VMEM((G_ROWS, 128), jnp.uint32) for _ in range(2 * group)]


def _peer_u(rows, tab, xt, g_t):
    t = xt.shape[1]
    n_steps = t // PEER_TOKENS
    col = pl.BlockSpec((PEER_SLOTS, PEER_TOKENS), lambda i: (0, i))
    return pl.pallas_call(
        _peer_u_kernel,
        grid=(n_steps,),
        in_specs=_idx_specs(n_steps) + [_table_spec(),
                                        pl.BlockSpec((D_MODEL, PEER_TOKENS), lambda i: (0, i)), col],
        out_specs=col,
        out_shape=jax.ShapeDtypeStruct((PEER_SLOTS, t), F32),
        scratch_shapes=[pltpu.VMEM((D_MODEL, PEER_TOKENS), F32),
                        pltpu.VMEM((PEER_SLOTS, PEER_TOKENS), F32)] + _gather_buffers(PEER_U_GROUP),
        compiler_params=_cparams(1),
        name="peer_u",
    )(rows, rows, tab, xt, g_t)


def _peer_v(rows, tab, wgt_t, x):
    t = x.shape[0]
    n_steps = t // PEER_TOKENS
    row = pl.BlockSpec((PEER_TOKENS, D_MODEL), lambda i: (i, 0))
    col = pl.BlockSpec((PEER_SLOTS, PEER_TOKENS), lambda i: (0, i))
    return pl.pallas_call(
        _peer_v_kernel,
        grid=(n_steps,),
        in_specs=_idx_specs(n_steps) + [_table_spec(), col, row],
        out_specs=row,
        out_shape=jax.ShapeDtypeStruct((t, D_MODEL), F32),
        scratch_shapes=[pltpu.VMEM((PEER_TOKENS, PEER_SLOTS), F32)] + _gather_buffers(PEER_V_GROUP),
        compiler_params=_cparams(1),
        name="peer_v",
    )(rows, rows, tab, wgt_t, x)


def _final_norm_kernel(x_ref, g_ref, y_ref):
    x = x_ref[...]
    ms = jnp.mean(x * x, axis=-1, keepdims=True)
    y_ref[...] = x * lax.rsqrt(ms + RMS_EPS) * g_ref[...]


def _final_norm(x, g, first_row, n_rows):
    tm = TOKEN_TILE
    base = first_row // tm
    return pl.pallas_call(
        _final_norm_kernel,
        grid=(n_rows // tm,),
        in_specs=[pl.BlockSpec((tm, D_MODEL), lambda i: (base + i, 0)), _const_spec((1, D_MODEL))],
        out_specs=pl.BlockSpec((tm, D_MODEL), lambda i: (i, 0)),
        out_shape=jax.ShapeDtypeStruct((n_rows, D_MODEL), F32),
        compiler_params=_cparams(1),
        name="final_norm",
    )(x, g)


def kernel(x_prompt, x_sample, cache_k, cache_v, state_ssm_re, state_ssm_im, norm1_g, w_in, ssm_a_re, ssm_a_im, ssm_log_dt, ssm_b_re, ssm_b_im, ssm_c_re, ssm_c_im, ssm_d, ssm_glu_w, attn_rel_bias, attn_w_o, w_out, norm2_g, peer_w_q, peer_sub_keys, peer_u, peer_v, final_g):
    bp, lp, _ = x_prompt.shape
    bs, ls, _ = x_sample.shape
    depth = w_in.shape[0]
    tp = bp * lp
    kv_win = cache_k.shape[2]
    keep = min(LEFT_CHUNKS * CHUNK, lp)
    x = jnp.concatenate([x_prompt.reshape(tp, D_MODEL), x_sample.reshape(bs * ls, D_MODEL)], axis=0)
    t = x.shape[0]
    assert t % TOKEN_TILE == 0 and lp % SCAN_CHUNK == 0 and lp % CHUNK == 0 and kv_win == LEFT_CHUNKS * CHUNK

    zeros_state = jnp.zeros((bp, N_SSM), F32)
    outs = {n: [] for n in ("p_re", "p_im", "p_k", "p_v", "s_re", "s_im", "s_k", "s_v")}
    for l in range(depth):
        u, q, kv, kvb, gates = _inproj(x, norm1_g[l][None], w_in[l].astype(BF16))

        ssm_args = (ssm_a_re[l], ssm_a_im[l], ssm_log_dt[l], ssm_b_re[l], ssm_b_im[l], ssm_c_re[l], ssm_c_im[l])
        dskip = ssm_d[l][None]
        bcat, apr, api, ccat = _ssm_params(*ssm_args, SCAN_CHUNK)
        y, hr_p, hi_p = _ssm(u, None, bp, lp, 0, zeros_state, zeros_state,
                             bcat, apr, api, ccat, dskip, SCAN_CHUNK)
        y, hr_s, hi_s = _ssm(u, y, bs, ls, tp, state_ssm_re[l].reshape(bs, N_SSM),
                             state_ssm_im[l].reshape(bs, N_SSM), bcat, apr[:ls], api[:ls], ccat, dskip, ls)

        pad = ((0, 0), (LEFT_CHUNKS * CHUNK, 0), (0, 0))
        kb_p = jnp.pad(kvb[:tp, :D_ATTN].reshape(bp, lp, D_ATTN), pad)
        vb_p = jnp.pad(kvb[:tp, D_ATTN:].reshape(bp, lp, D_ATTN), pad)
        w_p = (LEFT_CHUNKS + 1) * CHUNK
        att = _attention(q, None, bp, lp, 0, kb_p, vb_p,
                         _bias_table(attn_rel_bias[l], CHUNK, w_p), CHUNK, ATTN_CHUNKS_PER_STEP, w_p, True)
        kb_s = jnp.concatenate([cache_k[l].reshape(bs, kv_win, D_ATTN).astype(BF16),
                                kvb[tp:, :D_ATTN].reshape(bs, ls, D_ATTN)], axis=1)
        vb_s = jnp.concatenate([cache_v[l].reshape(bs, kv_win, D_ATTN).astype(BF16),
                                kvb[tp:, D_ATTN:].reshape(bs, ls, D_ATTN)], axis=1)
        w_s = kv_win + ls
        att = _attention(q, att, bs, ls, tp, kb_s, vb_s,
                         _bias_table(attn_rel_bias[l], ls, w_s), ls, 1, w_s, False)

        x1, xn_t, qp = _mix(x, y, att, gates, ssm_glu_w[l].astype(BF16), attn_w_o[l].astype(BF16),
                          w_out[l].astype(BF16), norm2_g[l][None], peer_w_q[l].astype(BF16))
        keys = peer_sub_keys[l].reshape(2 * PEER_HEADS, N_KEYS, PEER_HALF).astype(BF16)
        rows, g_t = _route(qp, keys)
        wgt_t = _peer_u(rows, _pack_table(peer_u[l]), xn_t, g_t)
        x = _peer_v(rows, _pack_table(peer_v[l]), wgt_t, x1)

        kv_p = jnp.stack([kv[(i + 1) * lp - keep:(i + 1) * lp] for i in range(bp)])
        kv_p = kv_p.reshape(bp, keep, 2, N_HEADS, HEAD_DIM)
        kv_s = kv[tp:].reshape(bs, ls, 2, N_HEADS, HEAD_DIM)
        outs["p_re"].append(hr_p.reshape(bp, N_GROUPS, SSM_STATE))
        outs["p_im"].append(hi_p.reshape(bp, N_GROUPS, SSM_STATE))
        outs["p_k"].append(kv_p[:, :, 0])
        outs["p_v"].append(kv_p[:, :, 1])
        outs["s_re"].append(hr_s.reshape(bs, N_GROUPS, SSM_STATE))
        outs["s_im"].append(hi_s.reshape(bs, N_GROUPS, SSM_STATE))
        outs["s_k"].append(kv_s[:, :, 0])
        outs["s_v"].append(kv_s[:, :, 1])

    y_p = _final_norm(x, final_g[None], 0, tp)
    y_s = _final_norm(x, final_g[None], tp, bs * ls)
    st = {n: jnp.stack(v) for n, v in outs.items()}
    return (y_p.reshape(bp, lp, D_MODEL), y_s.reshape(bs, ls, D_MODEL),
            st["p_re"], st["p_im"], st["p_k"], st["p_v"], st["s_re"], st["s_im"], st["s_k"], st["s_v"])
```

```python
import functools
import math

import jax
import jax.numpy as jnp
from jax import lax
from jax.experimental import pallas as pl
from jax.experimental.pallas import tpu as pltpu

F32 = jnp.float32
BF16 = jnp.bfloat16

D_MODEL = 1024
CHUNK = 64
D_SSM = 512
SSM_GROUP = 16
N_GROUPS = D_SSM // SSM_GROUP
SSM_STATE = 64
N_SSM = N_GROUPS * SSM_STATE
N_HEADS = 8
HEAD_DIM = 64
D_ATTN = N_HEADS * HEAD_DIM
LEFT_CHUNKS = 8
REL_CLIP = 128
ATTN_SCALE = HEAD_DIM ** -0.5
N_KEYS = 128
N_EXPERTS = N_KEYS * N_KEYS
PEER_HEADS = 8
PEER_DK = 256
PEER_HALF = PEER_DK // 2
PEER_TOPK = 16
PEER_SLOTS = PEER_HEADS * PEER_TOPK
RMS_EPS = 1e-6
D_IN = D_SSM + 3 * D_ATTN + 2 * D_MODEL

VMEM_LIMIT_BYTES = 56 * 1024 * 1024
TOKEN_TILE = 256
ROUTE_TILE = 128
ROUTE_HEADS_PER_STEP = 8
PEER_TOKENS = 128
PEER_U_GROUP = 8
PEER_V_GROUP = 16
U_DOT_TOKENS = 4
ATTN_CHUNKS_PER_STEP = 8
SCAN_CHUNK = 512
SCAN_ROWS = 8
NEG_INF = float("-inf")

ROW_WORDS = 4
G_STRIDE = 136
G_ROWS = ROW_WORDS * G_STRIDE
IDX_QUAD = 8

PEER_CAND = [(i, j) for i in range(PEER_TOPK) for j in range(PEER_TOPK) if (i + 1) * (j + 1) <= PEER_TOPK]
N_CAND_ROWS = 64


def _cparams(n_axes):
    return pltpu.CompilerParams(dimension_semantics=("arbitrary",) * n_axes,
                                vmem_limit_bytes=VMEM_LIMIT_BYTES)


def _const_spec(shape):
    return pl.BlockSpec(shape, lambda *_: (0,) * len(shape))


def _inproj_kernel(x_ref, g_ref, w_ref, u_ref, q_ref, kv_ref, kvb_ref, gate_ref):
    x = x_ref[...]
    ms = jnp.mean(x * x, axis=-1, keepdims=True)
    xn = (x * lax.rsqrt(ms + RMS_EPS) * g_ref[...]).astype(BF16)

    def proj(lo, hi):
        return jnp.dot(xn, w_ref[:, lo:hi], preferred_element_type=F32)

    u_ref[...] = proj(0, D_SSM).astype(BF16)
    q_ref[...] = proj(D_SSM, D_SSM + D_ATTN).astype(BF16)
    for c in range(2):
        lo = D_SSM + D_ATTN + c * D_ATTN
        kv = proj(lo, lo + D_ATTN)
        kv_ref[:, c * D_ATTN:(c + 1) * D_ATTN] = kv
        kvb_ref[:, c * D_ATTN:(c + 1) * D_ATTN] = kv.astype(BF16)
    for c in range(4):
        lo = D_SSM + 3 * D_ATTN + c * 512
        gate_ref[:, c * 512:(c + 1) * 512] = jax.nn.sigmoid(proj(lo, lo + 512)).astype(BF16)


def _inproj(x, g, w_bf):
    t = x.shape[0]
    tm = TOKEN_TILE
    row = lambda n: pl.BlockSpec((tm, n), lambda i: (i, 0))
    return pl.pallas_call(
        _inproj_kernel,
        grid=(t // tm,),
        in_specs=[row(D_MODEL), _const_spec((1, D_MODEL)), _const_spec((D_MODEL, D_IN))],
        out_specs=[row(D_SSM), row(D_ATTN), row(2 * D_ATTN), row(2 * D_ATTN), row(2 * D_MODEL)],
        out_shape=[jax.ShapeDtypeStruct((t, D_SSM), BF16), jax.ShapeDtypeStruct((t, D_ATTN), BF16),
                   jax.ShapeDtypeStruct((t, 2 * D_ATTN), F32), jax.ShapeDtypeStruct((t, 2 * D_ATTN), BF16),
                   jax.ShapeDtypeStruct((t, 2 * D_MODEL), BF16)],
        compiler_params=_cparams(1),
        name="inproj",
    )(x, g, w_bf)


def _ssm_kernel(lc, u_ref, bcat_ref, apr_ref, api_ref, ccat_ref, d_ref, h0r_ref, h0i_ref, *rest):
    y_ref, hr_ref, hi_ref, re_ref, im_ref, cr_ref, ci_ref = rest[-7:]
    c = pl.program_id(1)

    @pl.when(c == 0)
    def _():
        cr_ref[...] = h0r_ref[0]
        ci_ref[...] = h0i_ref[0]

    u = u_ref[...]
    hc, hs = D_SSM // 2, N_SSM // 2
    for half in range(2):
        uh = u[:, half * hc:(half + 1) * hc]
        st = slice(half * hs, (half + 1) * hs)
        re_ref[:, st] = jnp.dot(uh, bcat_ref[half * hc:(half + 1) * hc, st], preferred_element_type=F32)
        im_ref[:, st] = jnp.dot(uh, bcat_ref[half * hc:(half + 1) * hc, N_SSM + half * hs:N_SSM + (half + 1) * hs],
                                preferred_element_type=F32)

    row = lax.broadcasted_iota(jnp.int32, (SCAN_ROWS, 128), 0)

    def lane_tile(j, _):
        sl = pl.ds(pl.multiple_of(j * 128, 128), 128)
        steps = []
        s = 1
        while s < SCAN_ROWS:
            steps.append((s, apr_ref[s - 1:s, sl], api_ref[s - 1:s, sl], row >= s))
            s *= 2
        pr = apr_ref[0:SCAN_ROWS, sl]
        pi = api_ref[0:SCAN_ROWS, sl]
        c_r = cr_ref[:, sl]
        c_i = ci_ref[:, sl]
        for blk in range(lc // SCAN_ROWS):
            rows = slice(blk * SCAN_ROWS, (blk + 1) * SCAN_ROWS)
            hr = re_ref[rows, sl]
            hi = im_ref[rows, sl]
            for s, ar, ai, keep in steps:
                sr = jnp.where(keep, pltpu.roll(hr, s, 0), 0.0)
                si = jnp.where(keep, pltpu.roll(hi, s, 0), 0.0)
                hr, hi = hr + (ar * sr - ai * si), hi + (ar * si + ai * sr)
            hr, hi = hr + (pr * c_r - pi * c_i), hi + (pr * c_i + pi * c_r)
            re_ref[rows, sl] = hr
            im_ref[rows, sl] = hi
            c_r = hr[SCAN_ROWS - 1:SCAN_ROWS]
            c_i = hi[SCAN_ROWS - 1:SCAN_ROWS]
        cr_ref[:, sl] = c_r
        ci_ref[:, sl] = c_i
        return 0

    lax.fori_loop(0, N_SSM // 128, lane_tile, 0)

    hr_ref[0] = cr_ref[...]
    hi_ref[0] = ci_ref[...]
    for half in range(2):
        st = slice(half * hs, (half + 1) * hs)
        ch = slice(half * hc, (half + 1) * hc)
        y = jnp.dot(re_ref[:, st].astype(BF16), ccat_ref[half * hs:(half + 1) * hs, ch],
                    preferred_element_type=F32)
        y = y + jnp.dot(im_ref[:, st].astype(BF16), ccat_ref[N_SSM + half * hs:N_SSM + (half + 1) * hs, ch],
                        preferred_element_type=F32)
        y_ref[:, ch] = (y + d_ref[:, ch] * u[:, ch].astype(F32)).astype(BF16)


def _seq_rows_spec(rows, width, first_row, seq_len):
    per_seq = seq_len // rows
    base = first_row // rows
    return pl.BlockSpec((rows, width), lambda i, n: (base + i * per_seq + n, 0))


def _alias_prev(prev, n_inputs):
    if prev is None:
        return [], [], {}
    return [prev], [pl.BlockSpec(memory_space=pl.ANY)], {n_inputs: 0}


def _ssm(u, prev_y, b, l, first_row, h0r, h0i, bcat, apr, api, ccat, dskip, lc):
    st = jax.ShapeDtypeStruct((b, 1, N_SSM), F32)
    state_spec = pl.BlockSpec((1, 1, N_SSM), lambda i, c: (i, 0, 0))
    rows = _seq_rows_spec(lc, D_SSM, first_row, l)
    extra, extra_specs, aliases = _alias_prev(prev_y, 8)
    return pl.pallas_call(
        functools.partial(_ssm_kernel, lc),
        grid=(b, l // lc),
        in_specs=[rows, _const_spec((D_SSM, 2 * N_SSM)), _const_spec((lc, N_SSM)), _const_spec((lc, N_SSM)),
                  _const_spec((2 * N_SSM, D_SSM)), _const_spec((1, D_SSM)), state_spec, state_spec] + extra_specs,
        out_specs=[rows, state_spec, state_spec],
        out_shape=[jax.ShapeDtypeStruct(u.shape, BF16), st, st],
        scratch_shapes=[pltpu.VMEM((lc, N_SSM), F32), pltpu.VMEM((lc, N_SSM), F32),
                        pltpu.VMEM((1, N_SSM), F32), pltpu.VMEM((1, N_SSM), F32)],
        input_output_aliases=aliases,
        compiler_params=_cparams(2),
        name=f"ssm_scan_{lc}",
    )(u, bcat, apr, api, ccat, dskip, h0r.reshape(b, 1, N_SSM), h0i.reshape(b, 1, N_SSM), *extra)


def _ssm_params(a_re, a_im, log_dt, b_re, b_im, c_re, c_im, lc):
    a = lax.complex(a_re, a_im)
    adt = a * jnp.exp(log_dt)[:, None]
    a_bar = jnp.exp(adt)
    b_bar = ((a_bar - 1.0) / a)[..., None] * lax.complex(b_re, b_im)
    eye = jnp.eye(N_GROUPS, dtype=F32)
    bre = jnp.einsum("gpc,gh->gchp", jnp.real(b_bar), eye).reshape(D_SSM, N_SSM)
    bim = jnp.einsum("gpc,gh->gchp", jnp.imag(b_bar), eye).reshape(D_SSM, N_SSM)
    bcat = jnp.concatenate([bre, bim], axis=1).astype(BF16)
    cre = jnp.einsum("gcp,gh->gphc", c_re, eye).reshape(N_SSM, D_SSM)
    cim = jnp.einsum("gcp,gh->gphc", c_im, eye).reshape(N_SSM, D_SSM)
    ccat = jnp.concatenate([cre, -cim], axis=0).astype(BF16)
    steps = jnp.arange(1, lc + 1, dtype=F32)[:, None, None]
    apow = jnp.exp(adt[None] * steps).reshape(lc, N_SSM)
    return bcat, jnp.real(apow), jnp.imag(apow), ccat


def _attn_kernel(qc, per_step, w, masked, q_ref, k_ref, v_ref, bias_ref, *rest):
    o_ref = rest[-1]
    second = lax.broadcasted_iota(jnp.int32, (qc, 2 * HEAD_DIM), 1) >= HEAD_DIM
    for c in range(per_step):
        n = pl.program_id(1) * per_step + c
        qrows = slice(c * qc, (c + 1) * qc)
        start = pl.multiple_of(n * qc, qc)
        scores = []
        for pr in range(N_HEADS // 2):
            lanes = slice(pr * 2 * HEAD_DIM, (pr + 1) * 2 * HEAD_DIM)
            q2 = q_ref[qrows, lanes]
            zero = jnp.zeros_like(q2)
            qs = jnp.concatenate([jnp.where(second, zero, q2), jnp.where(second, q2, zero)], axis=0)
            k2 = k_ref[0, pl.ds(start, w), lanes]
            scores.append(lax.dot_general(qs, k2, (((1,), (1,)), ((), ())), preferred_element_type=F32))
        s = jnp.concatenate(scores, axis=0)
        s = s * ATTN_SCALE + bias_ref[...]
        if masked:
            col = lax.broadcasted_iota(jnp.int32, (N_HEADS * qc, w), 1)
            s = jnp.where(col + n * qc >= LEFT_CHUNKS * CHUNK, s, jnp.finfo(F32).min)
        m = jnp.max(s, axis=-1, keepdims=True)
        p = jnp.exp(s - m)
        p = p / jnp.sum(p, axis=-1, keepdims=True)
        p = p.astype(BF16)
        for pr in range(N_HEADS // 2):
            lanes = slice(pr * 2 * HEAD_DIM, (pr + 1) * 2 * HEAD_DIM)
            v2 = v_ref[0, pl.ds(start, w), lanes]
            r = jnp.dot(p[2 * pr * qc:(2 * pr + 2) * qc], v2, preferred_element_type=F32)
            o_ref[qrows, lanes] = jnp.where(second, r[qc:], r[:qc]).astype(BF16)


def _attention(q, prev_att, b, l, first_row, k, v, bias, qc, per_step, w, masked):
    lk = k.shape[1]
    kv_spec = pl.BlockSpec((1, lk, D_ATTN), lambda i, n: (i, 0, 0))
    rows = _seq_rows_spec(qc * per_step, D_ATTN, first_row, l)
    extra, extra_specs, aliases = _alias_prev(prev_att, 4)
    return pl.pallas_call(
        functools.partial(_attn_kernel, qc, per_step, w, masked),
        grid=(b, l // (qc * per_step)),
        in_specs=[rows, kv_spec, kv_spec, _const_spec((N_HEADS * qc, w))] + extra_specs,
        out_specs=rows,
        out_shape=jax.ShapeDtypeStruct(q.shape, BF16),
        input_output_aliases=aliases,
        compiler_params=_cparams(2),
        name=f"band_attn_{qc}",
    )(q, k, v, bias, *extra)


def _bias_table(rel_bias, qc, w):
    lo = LEFT_CHUNKS * CHUNK + (qc - 1) - REL_CLIP
    hi = (w + qc - 1) - lo - (2 * REL_CLIP + 1)
    ext = jnp.pad(rel_bias.astype(F32), ((0, 0), (lo, max(hi, 0))), mode="edge")
    rows = [ext[:, qc - 1 - i:qc - 1 - i + w] for i in range(qc)]
    return jnp.stack(rows, axis=1).reshape(N_HEADS * qc, w)


def _mix_kernel(x_ref, y_ref, att_ref, gate_ref, glu_ref, wo_ref, wout_ref, g2_ref, wq_ref,
                x1_ref, xnt_ref, qp_ref):
    glu = jnp.dot(y_ref[...], glu_ref[...], preferred_element_type=F32)
    a = glu[:, :D_MODEL] * jax.nn.sigmoid(glu[:, D_MODEL:])
    b = jnp.dot(att_ref[...], wo_ref[...], preferred_element_type=F32)
    mixed = gate_ref[:, :D_MODEL].astype(F32) * a + gate_ref[:, D_MODEL:].astype(F32) * b
    x1 = x_ref[...] + jnp.dot(mixed.astype(BF16), wout_ref[...], preferred_element_type=F32)
    x1_ref[...] = x1
    ms = jnp.mean(x1 * x1, axis=-1, keepdims=True)
    xn = x1 * lax.rsqrt(ms + RMS_EPS) * g2_ref[...]
    xnt_ref[...] = xn.T.astype(BF16)
    qp_ref[...] = jnp.dot(xn.astype(BF16), wq_ref[...], preferred_element_type=F32).astype(BF16)


def _mix(x, y, att, gates, glu_w, w_o, w_out, g2, w_q):
    t = x.shape[0]
    tm = TOKEN_TILE
    row = lambda n: pl.BlockSpec((tm, n), lambda i: (i, 0))
    dq = PEER_HEADS * PEER_DK
    return pl.pallas_call(
        _mix_kernel,
        grid=(t // tm,),
        in_specs=[row(D_MODEL), row(D_SSM), row(D_ATTN), row(2 * D_MODEL),
                  _const_spec((D_SSM, 2 * D_MODEL)), _const_spec((D_ATTN, D_MODEL)),
                  _const_spec((D_MODEL, D_MODEL)), _const_spec((1, D_MODEL)), _const_spec((D_MODEL, dq))],
        out_specs=[row(D_MODEL), pl.BlockSpec((D_MODEL, tm), lambda i: (0, i)), row(dq)],
        out_shape=[jax.ShapeDtypeStruct((t, D_MODEL), F32), jax.ShapeDtypeStruct((D_MODEL, t), BF16),
                   jax.ShapeDtypeStruct((t, dq), BF16)],
        compiler_params=_cparams(1),
        name="mix",
    )(x, y, att, gates, glu_w, w_o, w_out, g2, w_q)


def _top_sorted(quarters, quarter_ids, quarter_tags=None):
    v = list(quarters)
    cols = [list(quarter_ids)] + ([list(quarter_tags)] if quarter_tags is not None else [])
    ids = cols[0]

    def order(a, b, ids_ordered):
        first = v[a] >= v[b] if ids_ordered else (v[a] > v[b]) | ((v[a] == v[b]) & (ids[a] < ids[b]))
        v[a], v[b] = jnp.where(first, v[a], v[b]), jnp.where(first, v[b], v[a])
        for c in cols:
            c[a], c[b] = jnp.where(first, c[a], c[b]), jnp.where(first, c[b], c[a])

    order(0, 1, True)
    order(2, 3, True)
    order(0, 2, True)
    order(1, 3, True)
    order(1, 2, False)
    id_bound = float(2 ** 24)
    tops = [[] for _ in range(1 + len(cols))]
    for _ in range(PEER_TOPK):
        m = jnp.max(v[0], axis=0, keepdims=True)
        key = jnp.min(jnp.where(v[0] == m, ids[0], id_bound), axis=0, keepdims=True)
        won = ids[0] == key
        tops[0].append(m)
        tops[1].append(key)
        if quarter_tags is not None:
            tops[2].append(jnp.max(jnp.where(won, cols[1][0], -1.0), axis=0, keepdims=True))
        for level in range(3):
            v[level] = jnp.where(won, v[level + 1], v[level])
            for c in cols:
                c[level] = jnp.where(won, c[level + 1], c[level])
        v[3] = jnp.where(won, NEG_INF, v[3])
    return tops


def _route_kernel(qp_ref, keys_ref, rows_ref, g_ref, e_ref):
    tm = ROUTE_TILE
    nk, nc = N_KEYS // 4, N_CAND_ROWS // 4
    quarter = lax.broadcasted_iota(jnp.int32, (nk, tm), 0).astype(F32)
    key_ids = [quarter + float(i * nk) for i in range(4)]
    cand_ids = [quarter[:nc] + float(i * nc) for i in range(4)]
    sub_masks = [lax.broadcasted_iota(jnp.int32, (8, tm), 0) == r for r in range(8)]

    def stack_rows(rows, n_rows, fill):
        groups = []
        for first in range(0, n_rows, 8):
            acc = jnp.full((8, tm), fill, F32)
            for r, value in enumerate(rows[first:first + 8]):
                acc = jnp.where(sub_masks[r], value, acc)
            groups.append(acc)
        return jnp.concatenate(groups, axis=0)

    def head(h, _):
        vals, ids = [], []
        for z in range(2):
            hz = 2 * h + z
            q = qp_ref[:, pl.ds(pl.multiple_of(hz * PEER_HALF, PEER_HALF), PEER_HALF)]
            s = lax.dot_general(keys_ref[hz], q, (((1,), (1,)), ((), ())),
                                preferred_element_type=F32)
            v_z, i_z = _top_sorted([s[i * nk:(i + 1) * nk] for i in range(4)], key_ids)
            vals.append(v_z)
            ids.append(i_z)
        sums = [vals[0][i] + vals[1][j] for i, j in PEER_CAND]
        experts = [ids[0][i] * float(N_KEYS) + ids[1][j] for i, j in PEER_CAND]
        best_s, _, best_e = _top_sorted(
            [stack_rows(sums[i * nc:(i + 1) * nc], nc, NEG_INF) for i in range(4)], cand_ids,
            [stack_rows(experts[i * nc:(i + 1) * nc], nc, 0.0) for i in range(4)])
        best = stack_rows([jnp.exp(m - best_s[0]) for m in best_s], PEER_TOPK, 0.0)
        out_rows = pl.ds(pl.multiple_of(h * PEER_TOPK, PEER_TOPK), PEER_TOPK)
        e_ref[out_rows, :] = stack_rows(best_e, PEER_TOPK, 0.0)
        g_ref[out_rows, :] = best / jnp.sum(best, axis=0, keepdims=True)
        return 0

    def heads(i, _):
        for u in range(ROUTE_HEADS_PER_STEP):
            head(i * ROUTE_HEADS_PER_STEP + u, 0)
        return 0

    lax.fori_loop(0, PEER_HEADS // ROUTE_HEADS_PER_STEP, heads, 0)
    rows_ref[...] = e_ref[...].T.astype(jnp.int32) * ROW_WORDS


def _route(qp, keys_bf):
    t = qp.shape[0]
    tm = ROUTE_TILE
    out = pl.BlockSpec((PEER_SLOTS, tm), lambda i: (0, i))
    return pl.pallas_call(
        _route_kernel,
        grid=(t // tm,),
        in_specs=[pl.BlockSpec((tm, PEER_HEADS * PEER_DK), lambda i: (i, 0)),
                  _const_spec((2 * PEER_HEADS, N_KEYS, PEER_HALF))],
        out_specs=[pl.BlockSpec((tm, PEER_SLOTS), lambda i: (i, 0)), out],
        out_shape=[jax.ShapeDtypeStruct((t, PEER_SLOTS), jnp.int32),
                   jax.ShapeDtypeStruct((PEER_SLOTS, t), F32)],
        scratch_shapes=[pltpu.VMEM((PEER_SLOTS, tm), F32)],
        compiler_params=_cparams(1),
        name="peer_route",
    )(qp, keys_bf)


def _pack_table(tab):
    tb = tab.astype(BF16)
    hi = lax.bitcast_convert_type(tb[:, :512], jnp.uint16).astype(jnp.uint32)
    lo = lax.bitcast_convert_type(tb[:, 512:], jnp.uint16).astype(jnp.uint32)
    return ((hi << 16) | lo).reshape(N_EXPERTS * ROW_WORDS, 128)


def _gather_group(idx_ref, g, tab_ref, bufs):
    for u, buf in enumerate(bufs):
        for a in range(PEER_SLOTS // IDX_QUAD):
            quad = idx_ref.at[g * len(bufs) + u, pl.ds(a * IDX_QUAD, IDX_QUAD)]
            for b in range(IDX_QUAD):
                k = a * IDX_QUAD + b
                row = pl.multiple_of(quad[b], ROW_WORDS)
                buf[pl.ds(k, ROW_WORDS, stride=G_STRIDE), :] = tab_ref[pl.ds(row, ROW_WORDS), :]


def _lane_tile(buf_ref, j):
    word = buf_ref[j * G_STRIDE:j * G_STRIDE + PEER_SLOTS, :]
    hi = pltpu.bitcast(word & jnp.uint32(0xFFFF0000), F32)
    lo = pltpu.bitcast(word << 16, F32)
    return hi, lo


def _token_pipeline(idx_ref, next_idx_ref, tab_ref, bufs, compute_group, split_regions):
    group = len(bufs) // 2
    n_groups = PEER_TOKENS // group
    halves = (bufs[:group], bufs[group:])
    step = pl.program_id(0)

    @pl.when(step == 0)
    def _():
        _gather_group(idx_ref, 0, tab_ref, halves[0])

    def pair(g, half, src_ref, src_group):
        def run():
            _gather_group(src_ref, src_group, tab_ref, halves[1 - half])
            compute_group(g * group, halves[half])

        if split_regions:
            pl.when(step >= 0)(run)
        else:
            run()

    def body(i, _):
        for half in range(2):
            pair(2 * i + half, half, idx_ref, 2 * i + half + 1)
        return 0

    lax.fori_loop(0, n_groups // 2 - 1, body, 0)
    pair(n_groups - 2, 0, idx_ref, n_groups - 1)
    pair(n_groups - 1, 1, next_idx_ref, 0)


def _gathered_rows(buf):
    tiles = [_lane_tile(buf, j) for j in range(ROW_WORDS)]
    return jnp.concatenate([tl[0] for tl in tiles] + [tl[1] for tl in tiles], axis=1)


def _peer_u_kernel(idx_ref, next_idx_ref, tab_ref, xt_ref, g_ref, w_ref, xw_ref, act_ref, *bufs):
    lane = lax.broadcasted_iota(jnp.int32, (PEER_SLOTS, PEER_TOKENS), 1)
    xw_ref[...] = xt_ref[...].astype(F32)

    def compute_group(t0, bufs):
        for first in range(0, len(bufs), U_DOT_TOKENS):
            part = bufs[first:first + U_DOT_TOKENS]
            lhs = jnp.concatenate([_gathered_rows(b) for b in part], axis=0)
            r = jnp.dot(lhs, xw_ref[...], preferred_element_type=F32)
            for u in range(U_DOT_TOKENS):
                pltpu.store(act_ref, r[u * PEER_SLOTS:(u + 1) * PEER_SLOTS], mask=lane == t0 + first + u)

    _token_pipeline(idx_ref, next_idx_ref, tab_ref, bufs, compute_group, split_regions=True)
    act = act_ref[...]
    gelu = 0.5 * act * (1.0 + lax.erf(act * (1.0 / math.sqrt(2.0))))
    w_ref[...] = g_ref[...] * gelu


def _peer_v_kernel(idx_ref, next_idx_ref, tab_ref, wgt_ref, x_ref, o_ref, wt_ref, *bufs):
    sub = lax.broadcasted_iota(jnp.int32, (8, PEER_SLOTS), 0)
    wt_ref[...] = wgt_ref[...].T

    def compute_group(t0, bufs):
        for u, buf in enumerate(bufs):
            w = wt_ref[pl.ds(t0 + u, 1), :]
            w0 = w.astype(BF16).astype(F32)
            r1 = w - w0
            w1 = r1.astype(BF16).astype(F32)
            w2 = r1 - w1
            lhs = jnp.where(sub == 0, w0, jnp.where(sub == 1, w1, jnp.where(sub == 2, w2, 0.0)))
            r = jnp.dot(lhs, _gathered_rows(buf), preferred_element_type=F32)
            o_ref[pl.ds(t0 + u, 1), :] = x_ref[pl.ds(t0 + u, 1), :] + (r[0:1] + r[1:2] + r[2:3])

    _token_pipeline(idx_ref, next_idx_ref, tab_ref, bufs, compute_group, split_regions=False)


def _idx_specs(n_steps):
    shape = (PEER_TOKENS, PEER_SLOTS)
    return [pl.BlockSpec(shape, lambda i: (i, 0), memory_space=pltpu.SMEM),
            pl.BlockSpec(shape, lambda i: (jnp.minimum(i + 1, n_steps - 1), 0), memory_space=pltpu.SMEM)]


def _table_spec():
    return pl.BlockSpec((N_EXPERTS * ROW_WORDS, 128), lambda i: (0, 0), pipeline_mode=pl.Buffered(1))


def _gather_buffers(group):
    return [pltpu.VMEM((G_ROWS, 128), jnp.uint32) for _ in range(2 * group)]


def _peer_u(rows, tab, xt, g_t):
    t = xt.shape[1]
    n_steps = t // PEER_TOKENS
    col = pl.BlockSpec((PEER_SLOTS, PEER_TOKENS), lambda i: (0, i))
    return pl.pallas_call(
        _peer_u_kernel,
        grid=(n_steps,),
        in_specs=_idx_specs(n_steps) + [_table_spec(),
                                        pl.BlockSpec((D_MODEL, PEER_TOKENS), lambda i: (0, i)), col],
        out_specs=col,
        out_shape=jax.ShapeDtypeStruct((PEER_SLOTS, t), F32),
        scratch_shapes=[pltpu.VMEM((D_MODEL, PEER_TOKENS), F32),
                        pltpu.VMEM((PEER_SLOTS, PEER_TOKENS), F32)] + _gather_buffers(PEER_U_GROUP),
        compiler_params=_cparams(1),
        name="peer_u",
    )(rows, rows, tab, xt, g_t)


def _peer_v(rows, tab, wgt_t, x):
    t = x.shape[0]
    n_steps = t // PEER_TOKENS
    row = pl.BlockSpec((PEER_TOKENS, D_MODEL), lambda i: (i, 0))
    col = pl.BlockSpec((PEER_SLOTS, PEER_TOKENS), lambda i: (0, i))
    return pl.pallas_call(
        _peer_v_kernel,
        grid=(n_steps,),
        in_specs=_idx_specs(n_steps) + [_table_spec(), col, row],
        out_specs=row,
        out_shape=jax.ShapeDtypeStruct((t, D_MODEL), F32),
        scratch_shapes=[pltpu.VMEM((PEER_TOKENS, PEER_SLOTS), F32)] + _gather_buffers(PEER_V_GROUP),
        compiler_params=_cparams(1),
        name="peer_v",
    )(rows, rows, tab, wgt_t, x)


def _final_norm_kernel(x_ref, g_ref, y_ref):
    x = x_ref[...]
    ms = jnp.mean(x * x, axis=-1, keepdims=True)
    y_ref[...] = x * lax.rsqrt(ms + RMS_EPS) * g_ref[...]


def _final_norm(x, g, first_row, n_rows):
    tm = TOKEN_TILE
    base = first_row // tm
    return pl.pallas_call(
        _final_norm_kernel,
        grid=(n_rows // tm,),
        in_specs=[pl.BlockSpec((tm, D_MODEL), lambda i: (base + i, 0)), _const_spec((1, D_MODEL))],
        out_specs=pl.BlockSpec((tm, D_MODEL), lambda i: (i, 0)),
        out_shape=jax.ShapeDtypeStruct((n_rows, D_MODEL), F32),
        compiler_params=_cparams(1),
        name="final_norm",
    )(x, g)


def kernel(x_prompt, x_sample, cache_k, cache_v, state_ssm_re, state_ssm_im, norm1_g, w_in, ssm_a_re, ssm_a_im, ssm_log_dt, ssm_b_re, ssm_b_im, ssm_c_re, ssm_c_im, ssm_d, ssm_glu_w, attn_rel_bias, attn_w_o, w_out, norm2_g, peer_w_q, peer_sub_keys, peer_u, peer_v, final_g):
    bp, lp, _ = x_prompt.shape
    bs, ls, _ = x_sample.shape
    depth = w_in.shape[0]
    tp = bp * lp
    kv_win = cache_k.shape[2]
    keep = min(LEFT_CHUNKS * CHUNK, lp)
    x = jnp.concatenate([x_prompt.reshape(tp, D_MODEL), x_sample.reshape(bs * ls, D_MODEL)], axis=0)
    t = x.shape[0]
    assert t % TOKEN_TILE == 0 and lp % SCAN_CHUNK == 0 and lp % CHUNK == 0 and kv_win == LEFT_CHUNKS * CHUNK

    zeros_state = jnp.zeros((bp, N_SSM), F32)
    outs = {n: [] for n in ("p_re", "p_im", "p_k", "p_v", "s_re", "s_im", "s_k", "s_v")}
    for l in range(depth):
        u, q, kv, kvb, gates = _inproj(x, norm1_g[l][None], w_in[l].astype(BF16))

        ssm_args = (ssm_a_re[l], ssm_a_im[l], ssm_log_dt[l], ssm_b_re[l], ssm_b_im[l], ssm_c_re[l], ssm_c_im[l])
        dskip = ssm_d[l][None]
        bcat, apr, api, ccat = _ssm_params(*ssm_args, SCAN_CHUNK)
        y, hr_p, hi_p = _ssm(u, None, bp, lp, 0, zeros_state, zeros_state,
                             bcat, apr, api, ccat, dskip, SCAN_CHUNK)
        y, hr_s, hi_s = _ssm(u, y, bs, ls, tp, state_ssm_re[l].reshape(bs, N_SSM),
                             state_ssm_im[l].reshape(bs, N_SSM), bcat, apr[:ls], api[:ls], ccat, dskip, ls)

        pad = ((0, 0), (LEFT_CHUNKS * CHUNK, 0), (0, 0))
        kb_p = jnp.pad(kvb[:tp, :D_ATTN].reshape(bp, lp, D_ATTN), pad)
        vb_p = jnp.pad(kvb[:tp, D_ATTN:].reshape(bp, lp, D_ATTN), pad)
        w_p = (LEFT_CHUNKS + 1) * CHUNK
        att = _attention(q, None, bp, lp, 0, kb_p, vb_p,
                         _bias_table(attn_rel_bias[l], CHUNK, w_p), CHUNK, ATTN_CHUNKS_PER_STEP, w_p, True)
        kb_s = jnp.concatenate([cache_k[l].reshape(bs, kv_win, D_ATTN).astype(BF16),
                                kvb[tp:, :D_ATTN].reshape(bs, ls, D_ATTN)], axis=1)
        vb_s = jnp.concatenate([cache_v[l].reshape(bs, kv_win, D_ATTN).astype(BF16),
                                kvb[tp:, D_ATTN:].reshape(bs, ls, D_ATTN)], axis=1)
        w_s = kv_win + ls
        att = _attention(q, att, bs, ls, tp, kb_s, vb_s,
                         _bias_table(attn_rel_bias[l], ls, w_s), ls, 1, w_s, False)

        x1, xn_t, qp = _mix(x, y, att, gates, ssm_glu_w[l].astype(BF16), attn_w_o[l].astype(BF16),
                          w_out[l].astype(BF16), norm2_g[l][None], peer_w_q[l].astype(BF16))
        keys = peer_sub_keys[l].reshape(2 * PEER_HEADS, N_KEYS, PEER_HALF).astype(BF16)
        rows, g_t = _route(qp, keys)
        wgt_t = _peer_u(rows, _pack_table(peer_u[l]), xn_t, g_t)
        x = _peer_v(rows, _pack_table(peer_v[l]), wgt_t, x1)

        kv_p = jnp.stack([kv[(i + 1) * lp - keep:(i + 1) * lp] for i in range(bp)])
        kv_p = kv_p.reshape(bp, keep, 2, N_HEADS, HEAD_DIM)
        kv_s = kv[tp:].reshape(bs, ls, 2, N_HEADS, HEAD_DIM)
        outs["p_re"].append(hr_p.reshape(bp, N_GROUPS, SSM_STATE))
        outs["p_im"].append(hi_p.reshape(bp, N_GROUPS, SSM_STATE))
        outs["p_k"].append(kv_p[:, :, 0])
        outs["p_v"].append(kv_p[:, :, 1])
        outs["s_re"].append(hr_s.reshape(bs, N_GROUPS, SSM_STATE))
        outs["s_im"].append(hi_s.reshape(bs, N_GROUPS, SSM_STATE))
        outs["s_k"].append(kv_s[:, :, 0])
        outs["s_v"].append(kv_s[:, :, 1])

    y_p = _final_norm(x, final_g[None], 0, tp)
    y_s = _final_norm(x, final_g[None], tp, bs * ls)
    st = {n: jnp.stack(v) for n, v in outs.items()}
    return (y_p.reshape(bp, lp, D_MODEL), y_s.reshape(bs, ls, D_MODEL),
            st["p_re"], st["p_im"], st["p_k"], st["p_v"], st["s_re"], st["s_im"], st["s_k"], st["s_v"])
```
